```python
import jax, jax.numpy as jnp
from jax import lax
import numpy as np

D_MODEL = 1024
BATCH = 32
SEQ = 256
DEPTH = 2
DEC_BATCH = 2
DEC_SEQ = 4096
PAST_LEN = 512

GRID_W = 64
HEAD_DIM = 64
D_A = D_MODEL
N_HEADS_A = D_A // HEAD_DIM
R_W = 64
R_A = 64
R_G = 128
D_B = D_MODEL
D_FF = ((8 * D_MODEL // 3 + 127) // 128) * 128
N_DIR = 2
IN_SIZES = (D_A, D_A, D_A, R_W, R_W, R_A, R_A, R_G, D_B, D_B, D_B, 2 * D_MODEL)
IN_TOTAL = 3 * D_A + 2 * R_W + 2 * R_A + R_G + 3 * D_B + 2 * D_MODEL
NORM_EPS = 1e-6
GN_EPS = 64e-5
L2_EPS = 1e-12

kernel_name = 'bidir_rwkv7_shortconv_convffn_diffusion_step'


def rmsnorm(x, g):
    xf = x.astype(jnp.float32)
    y = xf * lax.rsqrt(jnp.mean(xf * xf, axis=-1, keepdims=True) + NORM_EPS)
    return (y * g.astype(jnp.float32)).astype(x.dtype)


def dwconv3(x, w, b, axis):
    n = x.shape[axis]
    pad = [(0, 0)] * x.ndim
    pad[axis] = (1, 1)
    xp = jnp.pad(x, pad)
    prev = lax.slice_in_dim(xp, 0, n, axis=axis)
    nxt = lax.slice_in_dim(xp, 2, n + 2, axis=axis)
    return prev * w[0] + x * w[1] + nxt * w[2] + b


def seq_conv(x, w, b, grid, along):
    if not grid:
        return dwconv3(x, w, b, 1)
    bsz, t, ch = x.shape
    rows = t // GRID_W
    xg = x.reshape(bsz, rows, GRID_W, ch)
    axis = 2 if along == 'row' else 1
    return dwconv3(xg, w, b, axis).reshape(bsz, t, ch)


def wkv_scan(s0, r, w, k, v, kk, a, reverse):
    xs = tuple(jnp.moveaxis(t, 1, 0) for t in (r, w, k, v, -kk, kk * a))

    def step(s, inp):
        r_t, w_t, k_t, v_t, a_t, b_t = inp
        sa = jnp.einsum('bhij,bhj->bhi', s, a_t)
        s = s * w_t[:, :, None, :] + sa[..., None] * b_t[:, :, None, :] + v_t[..., None] * k_t[:, :, None, :]
        return s, jnp.einsum('bhij,bhj->bhi', s, r_t)

    s_fin, ys = lax.scan(step, s0, xs, reverse=reverse)
    return jnp.moveaxis(ys, 0, 1), s_fin


def rwkv_mixer(r, k, v, wdf, wdb, adf, adb, gd, s0_f, s0_b, lp):
    bsz, t, _ = r.shape
    f32 = jnp.float32
    dt = r.dtype
    heads = lambda z: z.reshape(bsz, t, N_HEADS_A, HEAD_DIM)
    r, k, v = r.astype(f32), k.astype(f32), v.astype(f32)
    rh, vh = heads(r), heads(v)
    kk = heads(k * lp['k_k'].astype(f32))
    kk = kk / jnp.maximum(jnp.linalg.norm(kk, axis=-1, keepdims=True), L2_EPS)
    r_k = lp['r_k'].astype(f32)
    k_a = lp['k_a'].astype(f32)

    def direction(d, wdown, adown, s0, reverse):
        w_log = -jax.nn.softplus(-(lp['decay_w0'][d].astype(f32)
                                   + jnp.tanh(wdown.astype(f32)) @ lp['decay_w2'][d].astype(f32))) - 0.5
        decay = jnp.exp(-jnp.exp(w_log))
        a = jax.nn.sigmoid(lp['iclr_a0'][d].astype(f32) + adown.astype(f32) @ lp['iclr_a2'][d].astype(f32))
        k_d = heads(k * (1.0 + (a - 1.0) * k_a))
        y, s_fin = wkv_scan(s0.astype(f32), rh, heads(decay), k_d, vh, kk, heads(a), reverse)
        bonus = jnp.sum(rh * k_d * r_k, axis=-1, keepdims=True) * vh
        return y, bonus, s_fin

    y_f, bonus_f, s_f = direction(0, wdf, adf, s0_f, False)
    y_b, bonus_b, s_b = direction(1, wdb, adb, s0_b, True)
    y = y_f + y_b
    mu = jnp.mean(y, axis=-1, keepdims=True)
    var = jnp.mean(jnp.square(y - mu), axis=-1, keepdims=True)
    yn = ((y - mu) * lax.rsqrt(var + GN_EPS)).reshape(bsz, t, D_A)
    o = yn * lp['gn_w'].astype(f32) + lp['gn_b'].astype(f32) + (bonus_f + bonus_b).reshape(bsz, t, D_A)
    g = jax.nn.sigmoid(gd.astype(f32)) @ lp['gate_g2'].astype(f32)
    return (o * g).astype(dt), s_f, s_b


def trunk_layer(x, cond, s0_f, s0_b, grid, lp):
    dt = x.dtype
    mod = (jax.nn.silu(cond) @ lp['w_mod'] + lp['b_mod']).astype(dt)[:, None, :]
    sh1, sc1, gt1, sh2, sc2, gt2 = jnp.split(mod, 6, axis=-1)
    h = rmsnorm(x, lp['norm1_g']) * (1 + sc1) + sh1
    z = h @ lp['w_in']
    splits = [int(s) for s in np.cumsum(IN_SIZES)[:-1]]
    r, k, v, wdf, wdb, adf, adb, gd, cb, cc, cx, gates = jnp.split(z, splits, axis=-1)
    o_a, s_f, s_b = rwkv_mixer(r, k, v, wdf, wdb, adf, adb, gd, s0_f, s0_b, lp)
    o_b = cb * seq_conv(cc * cx, lp['conv_mix_w'], lp['conv_mix_b'], grid, 'row')
    g_a, g_b = jnp.split(jax.nn.sigmoid(gates), 2, axis=-1)
    merged = g_a * (o_a @ lp['w_pa']) + g_b * (o_b @ lp['w_pb'])
    x = x + gt1 * (merged @ lp['w_o'])
    h2 = rmsnorm(x, lp['norm2_g']) * (1 + sc2) + sh2
    u = seq_conv(h2 @ lp['w_up'], lp['conv_ffn_w'], lp['conv_ffn_b'], grid, 'col')
    u_act, u_lin = jnp.split(u, 2, axis=-1)
    x = x + gt2 * ((jax.nn.silu(u_act) * u_lin) @ lp['w_down'])
    return x, s_f, s_b


def setup_inputs(seed: int = 0) -> dict:
    key = jax.random.key(seed)
    ks = iter(jax.random.split(key, 32))
    nrm = lambda shape, s: jax.random.normal(next(ks), shape, jnp.float32) * s
    L, D = DEPTH, D_MODEL
    return {
        'x_prompt': nrm((BATCH, SEQ, D), 1.0),
        'x_sample': nrm((DEC_BATCH, DEC_SEQ, D), 1.0),
        'state_wkv': nrm((DEC_BATCH, DEPTH, N_DIR, N_HEADS_A, HEAD_DIM, HEAD_DIM), 0.3),
        'c': nrm((DEC_BATCH, D), 1.0),
        'c_ctx': nrm((D,), 1.0),
        'w_mod': nrm((L, D, 6 * D), 0.5 * D ** -0.5),
        'b_mod': nrm((L, 6 * D), 0.01),
        'norm1_g': 1.0 + nrm((L, D), 0.01),
        'w_in': nrm((L, D, IN_TOTAL), D ** -0.5),
        'decay_w0': nrm((L, N_DIR, D_A), 0.5) - 1.0,
        'decay_w2': nrm((L, N_DIR, R_W, D_A), 0.5 * R_W ** -0.5),
        'iclr_a0': nrm((L, N_DIR, D_A), 0.1),
        'iclr_a2': nrm((L, N_DIR, R_A, D_A), 0.5 * R_A ** -0.5),
        'gate_g2': nrm((L, R_G, D_A), R_G ** -0.5),
        'k_k': 0.85 + nrm((L, D_A), 0.1),
        'k_a': 1.0 + nrm((L, D_A), 0.1),
        'r_k': nrm((L, N_HEADS_A, HEAD_DIM), 0.1),
        'gn_w': 1.0 + nrm((L, D_A), 0.01),
        'gn_b': nrm((L, D_A), 0.01),
        'conv_mix_w': nrm((L, 3, D_B), 0.5),
        'conv_mix_b': nrm((L, D_B), 0.01),
        'w_pa': nrm((L, D_A, D), D_A ** -0.5),
        'w_pb': nrm((L, D_B, D), D_B ** -0.5),
        'w_o': nrm((L, D, D), D ** -0.5),
        'norm2_g': 1.0 + nrm((L, D), 0.01),
        'w_up': nrm((L, D, 2 * D_FF), D ** -0.5),
        'conv_ffn_w': nrm((L, 3, 2 * D_FF), 0.5),
        'conv_ffn_b': nrm((L, 2 * D_FF), 0.01),
        'w_down': nrm((L, D_FF, D), D_FF ** -0.5),
        'norm_f_g': 1.0 + nrm((D,), 0.01),
    }


def reference(x_prompt, x_sample, state_wkv, c, c_ctx, w_mod, b_mod, norm1_g, w_in, decay_w0, decay_w2,
              iclr_a0, iclr_a2, gate_g2, k_k, k_a, r_k, gn_w, gn_b, conv_mix_w, conv_mix_b, w_pa, w_pb, w_o,
              norm2_g, w_up, conv_ffn_w, conv_ffn_b, w_down, norm_f_g):
    lps = [dict(w_mod=w_mod[l], b_mod=b_mod[l], norm1_g=norm1_g[l], w_in=w_in[l], decay_w0=decay_w0[l],
                decay_w2=decay_w2[l], iclr_a0=iclr_a0[l], iclr_a2=iclr_a2[l], gate_g2=gate_g2[l], k_k=k_k[l],
                k_a=k_a[l], r_k=r_k[l], gn_w=gn_w[l], gn_b=gn_b[l], conv_mix_w=conv_mix_w[l],
                conv_mix_b=conv_mix_b[l], w_pa=w_pa[l], w_pb=w_pb[l], w_o=w_o[l], norm2_g=norm2_g[l],
                w_up=w_up[l], conv_ffn_w=conv_ffn_w[l], conv_ffn_b=conv_ffn_b[l], w_down=w_down[l])
           for l in range(DEPTH)]

    xp = x_prompt
    zeros = jnp.zeros((x_prompt.shape[0], N_HEADS_A, HEAD_DIM, HEAD_DIM), jnp.float32)
    ctx_states = []
    for l in range(DEPTH):
        xp, s_f, s_b = trunk_layer(xp, c_ctx[None, :], zeros, zeros, False, lps[l])
        ctx_states.append(jnp.stack([s_f, s_b], axis=1))
    new_state_wkv = jnp.stack(ctx_states, axis=1).astype(x_prompt.dtype)
    y_prompt = rmsnorm(xp, norm_f_g)

    xs = x_sample
    for l in range(DEPTH):
        xs, _, _ = trunk_layer(xs, c, state_wkv[:, l, 0], state_wkv[:, l, 1], True, lps[l])
    y_sample = rmsnorm(xs, norm_f_g)
    return (y_prompt, y_sample, new_state_wkv)
```

```python
import functools

import jax
import jax.numpy as jnp
from jax import lax
from jax.experimental import pallas as pl
from jax.experimental.pallas import tpu as pltpu

F32 = jnp.float32
BF16 = jnp.bfloat16
HIGHEST = lax.Precision.HIGHEST

HEAD_DIM = 64
PAIR = 2 * HEAD_DIM
CHUNK = 64
GRID_W = 64
R_SMALL = 256
R_G = 128
NORM_EPS = 1e-6
GN_EPS = 64e-5
L2_EPS = 1e-12
VMEM_LIMIT = 56 * 1024 * 1024


def _cparams():
    return pltpu.CompilerParams(vmem_limit_bytes=VMEM_LIMIT)


def _dot(a, b, hi=False):
    if hi:
        return jnp.dot(a, b, precision=HIGHEST, preferred_element_type=F32)
    return jnp.dot(a.astype(BF16), b.astype(BF16), preferred_element_type=F32)


def _dot_nt(a, b):
    return lax.dot_general(a.astype(BF16), b.astype(BF16), (((1,), (1,)), ((), ())),
                           preferred_element_type=F32)


def _dot_tn(a, b):
    return jnp.dot(a.T.astype(BF16), b.astype(BF16), preferred_element_type=F32)


def _sigmoid(x):
    return 1.0 / (1.0 + jnp.exp(-x))


def _silu(x):
    return x * _sigmoid(x)


def _mod_kernel(c_ref, w_ref, b_ref, o_ref):
    s = _silu(c_ref[...])
    o_ref[0] = _dot(s, w_ref[0], hi=True) + b_ref[0]


def _modulation(cond8, w_mod, b_mod):
    n_layers, d, n = w_mod.shape
    tn = n // 4
    return pl.pallas_call(
        _mod_kernel,
        grid=(n_layers, n // tn),
        in_specs=[pl.BlockSpec((8, d), lambda l, j: (0, 0)),
                  pl.BlockSpec((1, d, tn), lambda l, j: (l, 0, j)),
                  pl.BlockSpec((1, 1, tn), lambda l, j: (l, 0, j))],
        out_specs=pl.BlockSpec((1, 8, tn), lambda l, j: (l, 0, j)),
        out_shape=jax.ShapeDtypeStruct((n_layers, 8, n), F32),
        compiler_params=_cparams(),
        name="modulation",
    )(cond8, w_mod, b_mod.reshape(n_layers, 1, n))


def _mod_row(i, tm, seq_len, row0):
    if row0 == 0:
        return 0
    return row0 + (i * tm) // seq_len


def _norm_mm_kernel(x_ref, g_ref, mod_ref, w_ref, o_ref, h_ref, *, tm, seq_len, row0, sc_col, sh_col):
    i = pl.program_id(0)
    d = x_ref.shape[1]

    @pl.when(pl.program_id(1) == 0)
    def _():
        x = x_ref[...]
        y = x * lax.rsqrt(jnp.mean(x * x, axis=-1, keepdims=True) + NORM_EPS) * g_ref[...]
        row = _mod_row(i, tm, seq_len, row0)
        sc = mod_ref[pl.ds(row, 1), sc_col * d:(sc_col + 1) * d]
        sh = mod_ref[pl.ds(row, 1), sh_col * d:(sh_col + 1) * d]
        h_ref[...] = (y * (1.0 + sc) + sh).astype(BF16)

    o_ref[...] = jnp.dot(h_ref[...], w_ref[...], preferred_element_type=F32)


def _norm_mm(x, g, mod_l, w, *, tm, tn, seq_len, row0, sc_col, sh_col, name):
    t, d = x.shape
    n = w.shape[1]
    kern = functools.partial(_norm_mm_kernel, tm=tm, seq_len=seq_len, row0=row0, sc_col=sc_col, sh_col=sh_col)
    return pl.pallas_call(
        kern,
        grid=(t // tm, n // tn),
        in_specs=[pl.BlockSpec((tm, d), lambda i, j: (i, 0)),
                  pl.BlockSpec((1, d), lambda i, j: (0, 0)),
                  pl.BlockSpec(mod_l.shape, lambda i, j: (0, 0)),
                  pl.BlockSpec((d, tn), lambda i, j: (0, j))],
        out_specs=pl.BlockSpec((tm, tn), lambda i, j: (i, j)),
        out_shape=jax.ShapeDtypeStruct((t, n), F32),
        scratch_shapes=[pltpu.VMEM((tm, d), BF16)],
        compiler_params=_cparams(),
        name=name,
    )(x, g.reshape(1, d), mod_l, w)


def _wkv_kernel(r_ref, k_ref, v_ref, zs_ref, wdec_ref, wicl_ref, w0_ref, a0_ref, kk_ref, ka_ref, rk_ref, s0_ref,
                y_ref, bonus_ref, sfin_ref, state_ref, *, reverse, n_chunks):
    jt = pl.program_id(2)
    c = CHUNK

    @pl.when(jt == 0)
    def _():
        state_ref[...] = s0_ref[0, 0]

    r = r_ref[...]
    k = k_ref[...]
    v = v_ref[...]
    zs = zs_ref[...]

    lane = lax.broadcasted_iota(jnp.int32, (PAIR, PAIR), 1)
    row = lax.broadcasted_iota(jnp.int32, (PAIR, PAIR), 0)
    head_ones = ((lane // HEAD_DIM) == (row // HEAD_DIM)).astype(F32)
    if reverse:
        strict, incl = lane > row, lane >= row
    else:
        strict, incl = row > lane, row >= lane
    tri = incl[:c, :c].astype(F32)
    head0 = lax.broadcasted_iota(jnp.int32, (c, PAIR), 1) < HEAD_DIM

    w_lin = w0_ref[0] + _dot(jnp.tanh(zs), wdec_ref[0], hi=True)
    softplus = jnp.maximum(-w_lin, 0.0) + jnp.log(1.0 + jnp.exp(-jnp.abs(w_lin)))
    w_log = -softplus - 0.5
    log_decay = -jnp.exp(w_log)
    a_lr = _sigmoid(a0_ref[0] + _dot(zs, wicl_ref[0], hi=True))
    kk = k * kk_ref[...]
    kk_norm = jnp.sqrt(_dot(kk * kk, head_ones, hi=True))
    kk = kk / jnp.maximum(kk_norm, L2_EPS)
    k_d = k * (1.0 + (a_lr - 1.0) * ka_ref[...])
    bonus_ref[0] = _dot(r * k_d * rk_ref[...], head_ones, hi=True) * v
    a_vec = -kk
    b_vec = kk * a_lr

    def stack(x):
        return jnp.concatenate([jnp.where(head0, x, 0.0), jnp.where(head0, 0.0, x)], axis=0)

    local = []
    for ci in range(n_chunks):
        sl = slice(ci * c, (ci + 1) * c)
        lw = log_decay[sl]
        cum = _dot(tri, lw, hi=True)
        tot = cum[0:1] if reverse else cum[c - 1:c]
        g_in = jnp.exp(cum)
        g_ex = jnp.exp(cum - lw)
        g_inv = jnp.exp(-cum)
        g_hat = jnp.exp(tot - cum)
        aa = stack(a_vec[sl] * g_ex)
        ra = stack(r[sl] * g_in)
        bb = stack(b_vec[sl] * g_inv)
        kt = stack(k_d[sl] * g_inv)
        bh = stack(b_vec[sl] * g_hat)
        kh = stack(k_d[sl] * g_hat)
        vm = stack(v[sl])
        scores = _dot_nt(jnp.concatenate([aa, ra], axis=0), jnp.concatenate([bb, kt], axis=0))
        n_ab = jnp.where(strict, scores[:PAIR, :PAIR], 0.0)
        n_ak = jnp.where(strict, scores[:PAIR, PAIR:], 0.0)
        m_rb = jnp.where(incl, scores[PAIR:, :PAIR], 0.0)
        m_rk = jnp.where(incl, scores[PAIR:, PAIR:], 0.0)
        x = jnp.concatenate([aa, _dot(n_ak, vm)], axis=1)
        n_pow = n_ab
        steps = CHUNK.bit_length() - 1
        for si in range(steps):
            x = x + _dot(n_pow, x)
            if si + 1 < steps:
                n_pow = _dot(n_pow, n_pow)
        ah, ul = x[:, :PAIR], x[:, PAIR:]
        pp = _dot_tn(ah, bh)
        qq = _dot_tn(jnp.concatenate([ul, vm], axis=0), jnp.concatenate([bh, kh], axis=0))
        rh = ra + _dot(m_rb, ah)
        y_loc = _dot(m_rb, ul) + _dot(m_rk, vm)
        local.append((jnp.exp(tot), pp, qq, rh, y_loc))

    state = state_ref[...]
    ys = [None] * n_chunks
    order = range(n_chunks - 1, -1, -1) if reverse else range(n_chunks)
    for ci in order:
        g_c, pp, qq, rh, y_loc = local[ci]
        ym = y_loc + _dot_nt(rh, state)
        state = state * g_c + _dot(state, pp) + qq
        ys[ci] = ym[:c] + ym[c:]
    y_ref[0] = jnp.concatenate(ys, axis=0)
    state_ref[...] = state

    @pl.when(jt == pl.num_programs(2) - 1)
    def _():
        sfin_ref[0, 0] = state


def _wkv(z, s0p, wdec, wicl, w0, a0, k_k, k_a, r_k, *, n_seq, seq_len, tb, reverse, col_r, col_k, col_v, col_s):
    t = z.shape[0]
    n_pairs = k_k.shape[1] // PAIR
    nb = seq_len // tb
    d = 1 if reverse else 0

    def tblk(b, j):
        return b * nb + ((nb - 1 - j) if reverse else j)

    kern = functools.partial(_wkv_kernel, reverse=reverse, n_chunks=tb // CHUNK)
    tok = lambda col: pl.BlockSpec((tb, PAIR), lambda b, p, j: (tblk(b, j), col // PAIR + p))
    vec = pl.BlockSpec((1, PAIR), lambda b, p, j: (0, p))
    dvec = pl.BlockSpec((1, 1, PAIR), lambda b, p, j: (d, 0, p))
    dmat = pl.BlockSpec((1, R_SMALL, PAIR), lambda b, p, j: (d, 0, p))
    st = pl.BlockSpec((1, 1, PAIR, PAIR), lambda b, p, j: (b, p, 0, 0))
    out_tok = pl.BlockSpec((1, tb, PAIR), lambda b, p, j: (0, tblk(b, j), p))
    y, bonus, sfin = pl.pallas_call(
        kern,
        grid=(n_seq, n_pairs, nb),
        in_specs=[tok(col_r), tok(col_k), tok(col_v),
                  pl.BlockSpec((tb, R_SMALL), lambda b, p, j: (tblk(b, j), col_s // R_SMALL)),
                  dmat, dmat, dvec, dvec, vec, vec, vec, st],
        out_specs=[out_tok, out_tok, st],
        out_shape=[jax.ShapeDtypeStruct((1, t, n_pairs * PAIR), F32),
                   jax.ShapeDtypeStruct((1, t, n_pairs * PAIR), F32),
                   jax.ShapeDtypeStruct((n_seq, n_pairs, PAIR, PAIR), F32)],
        scratch_shapes=[pltpu.VMEM((PAIR, PAIR), F32)],
        compiler_params=_cparams(),
        name="wkv_bwd" if reverse else "wkv_fwd",
    )(z, z, z, z, wdec, wicl, w0, a0, k_k, k_a, r_k, s0p)
    return y[0], bonus[0], sfin


def _shift_prev(x, period):
    t = lax.broadcasted_iota(jnp.int32, x.shape, 0)
    return jnp.where((t & (period - 1)) == 0, 0.0, pltpu.roll(x, 1, 0))


def _shift_next(x, period):
    t = lax.broadcasted_iota(jnp.int32, x.shape, 0)
    return jnp.where((t & (period - 1)) == period - 1, 0.0, pltpu.roll(x, x.shape[0] - 1, 0))


def _mix_kernel(x_ref, yf_ref, yb_ref, bf_ref, bb_ref, gd_ref, cb_ref, cc_ref, cx_ref, ga_ref, gb_ref, mod_ref,
                gnw_ref, gnb_ref, g2_ref, cw_ref, cbias_ref, wpa_ref, wpb_ref, wo_ref, o_ref,
                *, tm, seq_len, row0, period):
    i = pl.program_id(0)
    d = x_ref.shape[1]
    lane = lax.broadcasted_iota(jnp.int32, (PAIR, PAIR), 1)
    row = lax.broadcasted_iota(jnp.int32, (PAIR, PAIR), 0)
    head_mean = jnp.where((lane // HEAD_DIM) == (row // HEAD_DIM), 1.0 / HEAD_DIM, 0.0).astype(F32)

    y = yf_ref[...] + yb_ref[...]
    parts = []
    for s in range(d // PAIR):
        ys = y[:, s * PAIR:(s + 1) * PAIR]
        mu = _dot(ys, head_mean, hi=True)
        yc = ys - mu
        var = _dot(yc * yc, head_mean, hi=True)
        parts.append(yc * lax.rsqrt(var + GN_EPS))
    yn = jnp.concatenate(parts, axis=1)
    o = yn * gnw_ref[...] + gnb_ref[...] + bf_ref[...] + bb_ref[...]
    gate = _dot(_sigmoid(gd_ref[...]), g2_ref[...])
    o_a = o * gate

    p = cc_ref[...] * cx_ref[...]
    cw = cw_ref[...]
    conv = _shift_prev(p, period) * cw[0:1] + p * cw[1:2] + _shift_next(p, period) * cw[2:3] + cbias_ref[...]
    o_b = cb_ref[...] * conv

    merged = _sigmoid(ga_ref[...]) * _dot(o_a, wpa_ref[...]) + _sigmoid(gb_ref[...]) * _dot(o_b, wpb_ref[...])
    mrow = _mod_row(i, tm, seq_len, row0)
    gt1 = mod_ref[pl.ds(mrow, 1), 2 * d:3 * d]
    o_ref[...] = x_ref[...] + gt1 * _dot(merged, wo_ref[...])


def _mix(x, yf, yb, bonf, bonb, z, mod_l, gn_w, gn_b, g2, conv_w, conv_b, w_pa, w_pb, w_o,
         *, tm, seq_len, row0, period, cols):
    t, d = x.shape
    kern = functools.partial(_mix_kernel, tm=tm, seq_len=seq_len, row0=row0, period=period)
    tok = pl.BlockSpec((tm, d), lambda i: (i, 0))
    zcol = lambda col: pl.BlockSpec((tm, d), lambda i: (i, col // d))
    full = lambda a: pl.BlockSpec(a.shape, lambda i: (0,) * a.ndim)
    vec = pl.BlockSpec((1, d), lambda i: (0, 0))
    return pl.pallas_call(
        kern,
        grid=(t // tm,),
        in_specs=[tok, tok, tok, tok, tok,
                  pl.BlockSpec((tm, R_G), lambda i: (i, cols["gd"] // R_G)),
                  zcol(cols["cb"]), zcol(cols["cc"]), zcol(cols["cx"]), zcol(cols["ga"]), zcol(cols["gb"]),
                  full(mod_l), vec, vec, full(g2), full(conv_w), vec, full(w_pa), full(w_pb), full(w_o)],
        out_specs=tok,
        out_shape=jax.ShapeDtypeStruct((t, d), F32),
        compiler_params=_cparams(),
        name="mix",
    )(x, yf, yb, bonf, bonb, z, z, z, z, z, z, mod_l, gn_w.reshape(1, d), gn_b.reshape(1, d), g2, conv_w,
      conv_b.reshape(1, d), w_pa, w_pb, w_o)


def _ffn_kernel(x_ref, u_ref, up_ref, un_ref, mod_ref, cw_ref, cb_ref, wd_ref, gf_ref, o_ref,
                *, tm, seq_len, row0, grid, final, kc):
    i = pl.program_id(0)
    d = x_ref.shape[1]
    d_ff = wd_ref.shape[0]
    tiles_per_seq = seq_len // tm
    first = (i % tiles_per_seq) == 0
    last = (i % tiles_per_seq) == tiles_per_seq - 1

    def conv(col):
        u = u_ref[:, col:col + kc]
        if grid:
            halo_p = jnp.where(first, 0.0, up_ref[:, col:col + kc])
            halo_n = jnp.where(last, 0.0, un_ref[:, col:col + kc])
            prev = jnp.concatenate([halo_p, u[:tm - GRID_W]], axis=0)
            nxt = jnp.concatenate([u[GRID_W:], halo_n], axis=0)
        else:
            prev = _shift_prev(u, seq_len)
            nxt = _shift_next(u, seq_len)
        cw = cw_ref[:, col:col + kc]
        return prev * cw[0:1] + u * cw[1:2] + nxt * cw[2:3] + cb_ref[:, col:col + kc]

    acc = jnp.zeros((tm, d), F32)
    for ci in range(d_ff // kc):
        act = conv(ci * kc)
        lin = conv(d_ff + ci * kc)
        acc = acc + _dot(_silu(act) * lin, wd_ref[ci * kc:(ci + 1) * kc, :])
    mrow = _mod_row(i, tm, seq_len, row0)
    gt2 = mod_ref[pl.ds(mrow, 1), 5 * d:6 * d]
    xn = x_ref[...] + gt2 * acc
    if final:
        xn = xn * lax.rsqrt(jnp.mean(xn * xn, axis=-1, keepdims=True) + NORM_EPS) * gf_ref[...]
    o_ref[...] = xn


def _ffn(x, u, mod_l, conv_w, conv_b, w_down, norm_f_g, *, tm, seq_len, row0, grid, final):
    t, d = x.shape
    n_up = u.shape[1]
    hb = tm // GRID_W
    n_halo = t // GRID_W
    kern = functools.partial(_ffn_kernel, tm=tm, seq_len=seq_len, row0=row0, grid=grid, final=final, kc=256)
    full = lambda a: pl.BlockSpec(a.shape, lambda i: (0,) * a.ndim)
    return pl.pallas_call(
        kern,
        grid=(t // tm,),
        in_specs=[pl.BlockSpec((tm, d), lambda i: (i, 0)),
                  pl.BlockSpec((tm, n_up), lambda i: (i, 0)),
                  pl.BlockSpec((GRID_W, n_up), lambda i: (jnp.maximum(i * hb - 1, 0), 0)),
                  pl.BlockSpec((GRID_W, n_up), lambda i: (jnp.minimum(i * hb + hb, n_halo - 1), 0)),
                  full(mod_l), full(conv_w), pl.BlockSpec((1, n_up), lambda i: (0, 0)), full(w_down),
                  pl.BlockSpec((1, d), lambda i: (0, 0))],
        out_specs=pl.BlockSpec((tm, d), lambda i: (i, 0)),
        out_shape=jax.ShapeDtypeStruct((t, d), F32),
        compiler_params=_cparams(),
        name="ffn",
    )(x, u, u, u, mod_l, conv_w, conv_b.reshape(1, n_up), w_down, norm_f_g.reshape(1, d))


def _pack_state(s):
    b, h, n, _ = s.shape
    eye = jnp.eye(2, dtype=s.dtype)
    return jnp.einsum("bphij,hg->bphigj", s.reshape(b, h // 2, 2, n, n), eye).reshape(b, h // 2, 2 * n, 2 * n)


def _unpack_state(sp):
    b, p, n2, _ = sp.shape
    n = n2 // 2
    eye = jnp.eye(2, dtype=sp.dtype)
    return jnp.einsum("bphigj,hg->bphij", sp.reshape(b, p, 2, n, 2, n), eye).reshape(b, 2 * p, n, n)


def kernel(x_prompt, x_sample, state_wkv, c, c_ctx, w_mod, b_mod, norm1_g, w_in, decay_w0, decay_w2, iclr_a0,
           iclr_a2, gate_g2, k_k, k_a, r_k, gn_w, gn_b, conv_mix_w, conv_mix_b, w_pa, w_pb, w_o, norm2_g, w_up,
           conv_ffn_w, conv_ffn_b, w_down, norm_f_g):
    n_ctx, ctx_len, d = x_prompt.shape
    n_lat, lat_len, _ = x_sample.shape
    n_layers = w_mod.shape[0]
    d_a = k_k.shape[1]
    n_heads = d_a // HEAD_DIM
    r_w = decay_w2.shape[2]
    assert d_a == d and 4 * r_w == R_SMALL and gate_g2.shape[1] == R_G

    o_small = 3 * d_a
    o_gd = o_small + R_SMALL
    o_cb = o_gd + R_G
    o_gates = o_cb + 3 * d
    cols = dict(r=0, k=d, v=2 * d, cb=3 * d, cc=4 * d, cx=5 * d, ga=6 * d, gb=7 * d, small=8 * d,
                gd=8 * d + R_SMALL)
    n_in = 8 * d + R_SMALL + R_G
    n_in_pad = 8 * d + 512
    w_in_p = jnp.concatenate(
        [w_in[:, :, :o_small], w_in[:, :, o_cb:o_gates], w_in[:, :, o_gates:], w_in[:, :, o_small:o_cb],
         jnp.zeros((n_layers, d, n_in_pad - n_in), w_in.dtype)], axis=2).astype(BF16)
    w_up_b = w_up.astype(BF16)
    w_pa_b, w_pb_b, w_o_b = w_pa.astype(BF16), w_pb.astype(BF16), w_o.astype(BF16)
    w_down_b = w_down.astype(BF16)
    g2_b = gate_g2.astype(BF16)

    zrow = jnp.zeros((n_layers, r_w, d_a), F32)
    wdec = jnp.stack([jnp.concatenate([decay_w2[:, 0], zrow, zrow, zrow], axis=1),
                      jnp.concatenate([zrow, decay_w2[:, 1], zrow, zrow], axis=1)], axis=1)
    wicl = jnp.stack([jnp.concatenate([zrow, zrow, iclr_a2[:, 0], zrow], axis=1),
                      jnp.concatenate([zrow, zrow, zrow, iclr_a2[:, 1]], axis=1)], axis=1)

    cond8 = jnp.concatenate([c_ctx[None, :], c, jnp.zeros((8 - 1 - n_lat, d), F32)], axis=0)
    mod = _modulation(cond8, w_mod, b_mod)

    def layer(x, l, *, n_seq, seq_len, row0, grid, s0, final):
        tm = 256
        z = _norm_mm(x, norm1_g[l], mod[l], w_in_p[l], tm=512, tn=n_in_pad // 4, seq_len=seq_len, row0=row0,
                     sc_col=1, sh_col=0, name="in_proj")
        outs = []
        for rev in (False, True):
            dd = 1 if rev else 0
            outs.append(_wkv(z, s0[dd], wdec[l], wicl[l], decay_w0[l].reshape(2, 1, d_a),
                             iclr_a0[l].reshape(2, 1, d_a), k_k[l].reshape(1, d_a), k_a[l].reshape(1, d_a),
                             r_k[l].reshape(1, d_a), n_seq=n_seq, seq_len=seq_len, tb=256, reverse=rev,
                             col_r=cols["r"], col_k=cols["k"], col_v=cols["v"], col_s=cols["small"]))
        (yf, bonf, sf), (yb, bonb, sb) = outs
        x = _mix(x, yf, yb, bonf, bonb, z, mod[l], gn_w[l], gn_b[l], g2_b[l], conv_mix_w[l], conv_mix_b[l],
                 w_pa_b[l], w_pb_b[l], w_o_b[l], tm=tm, seq_len=seq_len, row0=row0,
                 period=GRID_W if grid else seq_len, cols=cols)
        u = _norm_mm(x, norm2_g[l], mod[l], w_up_b[l], tm=512, tn=w_up.shape[2] // 4, seq_len=seq_len, row0=row0,
                     sc_col=4, sh_col=3, name="ffn_up")
        x = _ffn(x, u, mod[l], conv_ffn_w[l], conv_ffn_b[l], w_down_b[l], norm_f_g, tm=tm, seq_len=seq_len,
                 row0=row0, grid=grid, final=final)
        return x, sf, sb

    xp = x_prompt.reshape(n_ctx * ctx_len, d)
    zero_state = jnp.zeros((n_ctx, n_heads // 2, PAIR, PAIR), F32)
    ctx_states = []
    for l in range(n_layers):
        xp, sf, sb = layer(xp, l, n_seq=n_ctx, seq_len=ctx_len, row0=0, grid=False,
                           s0=(zero_state, zero_state), final=(l == n_layers - 1))
        ctx_states.append(jnp.stack([_unpack_state(sf), _unpack_state(sb)], axis=1))
    new_state = jnp.stack(ctx_states, axis=1).astype(x_prompt.dtype)
    y_prompt = xp.reshape(n_ctx, ctx_len, d)

    xs = x_sample.reshape(n_lat * lat_len, d)
    for l in range(n_layers):
        s0 = (_pack_state(state_wkv[:, l, 0].astype(F32)), _pack_state(state_wkv[:, l, 1].astype(F32)))
        xs, _, _ = layer(xs, l, n_seq=n_lat, seq_len=lat_len, row0=1, grid=True, s0=s0,
                         final=(l == n_layers - 1))
    y_sample = xs.reshape(n_lat, lat_len, d)
    return (y_prompt, y_sample, new_state)
```

```python
import functools

import jax
import jax.numpy as jnp
from jax import lax
from jax.experimental import pallas as pl
from jax.experimental.pallas import tpu as pltpu

F32 = jnp.float32
BF16 = jnp.bfloat16
HIGHEST = lax.Precision.HIGHEST

HEAD_DIM = 64
PAIR = 2 * HEAD_DIM
CHUNK = 64
WKV_PAIRS_PER_STEP = 2
GRID_W = 64
R_SMALL = 256
R_G = 128
NORM_EPS = 1e-6
GN_EPS = 64e-5
L2_EPS = 1e-12
VMEM_LIMIT = 56 * 1024 * 1024


def _cparams():
    return pltpu.CompilerParams(vmem_limit_bytes=VMEM_LIMIT)


def _dot(a, b, hi=False):
    if hi:
        return jnp.dot(a, b, precision=HIGHEST, preferred_element_type=F32)
    return jnp.dot(a.astype(BF16), b.astype(BF16), preferred_element_type=F32)


def _dot_nt(a, b):
    return lax.dot_general(a.astype(BF16), b.astype(BF16), (((1,), (1,)), ((), ())),
                           preferred_element_type=F32)


def _dot_tn(a, b):
    return jnp.dot(a.T.astype(BF16), b.astype(BF16), preferred_element_type=F32)


def _split(x):
    hi = x.astype(BF16)
    return hi, (x - hi.astype(F32)).astype(BF16)


def _dot_split(a, b):
    hi, lo = _split(a)
    bb = b.astype(BF16)
    return jnp.dot(hi, bb, preferred_element_type=F32) + jnp.dot(lo, bb, preferred_element_type=F32)


def _dot_split_rhs(a, b):
    hi, lo = _split(b)
    ab = a.astype(BF16)
    return jnp.dot(ab, hi, preferred_element_type=F32) + jnp.dot(ab, lo, preferred_element_type=F32)


def _sigmoid(x):
    return 1.0 / (1.0 + jnp.exp(-x))


def _silu(x):
    return x * _sigmoid(x)


def _mod_kernel(c_ref, w_ref, b_ref, o_ref):
    s = _silu(c_ref[...])
    o_ref[0] = _dot(s, w_ref[0], hi=True) + b_ref[0]


def _modulation(cond8, w_mod, b_mod):
    n_layers, d, n = w_mod.shape
    tn = n // 4
    return pl.pallas_call(
        _mod_kernel,
        grid=(n_layers, n // tn),
        in_specs=[pl.BlockSpec((8, d), lambda l, j: (0, 0)),
                  pl.BlockSpec((1, d, tn), lambda l, j: (l, 0, j)),
                  pl.BlockSpec((1, 1, tn), lambda l, j: (l, 0, j))],
        out_specs=pl.BlockSpec((1, 8, tn), lambda l, j: (l, 0, j)),
        out_shape=jax.ShapeDtypeStruct((n_layers, 8, n), F32),
        compiler_params=_cparams(),
        name="modulation",
    )(cond8, w_mod, b_mod.reshape(n_layers, 1, n))


def _mod_row(i, tm, seq_len, row0):
    if row0 == 0:
        return 0
    return row0 + (i * tm) // seq_len


def _norm_mm_kernel(x_ref, g_ref, mod_ref, w_ref, o_ref, h_ref, *, tm, seq_len, row0, sc_col, sh_col):
    i = pl.program_id(0)
    d = x_ref.shape[1]

    @pl.when(pl.program_id(1) == 0)
    def _():
        x = x_ref[...]
        y = x * lax.rsqrt(jnp.mean(x * x, axis=-1, keepdims=True) + NORM_EPS) * g_ref[...]
        row = _mod_row(i, tm, seq_len, row0)
        sc = mod_ref[pl.ds(row, 1), sc_col * d:(sc_col + 1) * d]
        sh = mod_ref[pl.ds(row, 1), sh_col * d:(sh_col + 1) * d]
        h_ref[...] = (y * (1.0 + sc) + sh).astype(BF16)

    o_ref[...] = jnp.dot(h_ref[...], w_ref[...], preferred_element_type=F32)


def _norm_mm(x, g, mod_l, w, *, tm, tn, seq_len, row0, sc_col, sh_col, name):
    t, d = x.shape
    n = w.shape[1]
    kern = functools.partial(_norm_mm_kernel, tm=tm, seq_len=seq_len, row0=row0, sc_col=sc_col, sh_col=sh_col)
    return pl.pallas_call(
        kern,
        grid=(t // tm, n // tn),
        in_specs=[pl.BlockSpec((tm, d), lambda i, j: (i, 0)),
                  pl.BlockSpec((1, d), lambda i, j: (0, 0)),
                  pl.BlockSpec(mod_l.shape, lambda i, j: (0, 0)),
                  pl.BlockSpec((d, tn), lambda i, j: (0, j))],
        out_specs=pl.BlockSpec((tm, tn), lambda i, j: (i, j)),
        out_shape=jax.ShapeDtypeStruct((t, n), F32),
        scratch_shapes=[pltpu.VMEM((tm, d), BF16)],
        compiler_params=_cparams(),
        name=name,
    )(x, g.reshape(1, d), mod_l, w)


def _wkv_kernel(r_ref, k_ref, v_ref, zs_ref, wdec_ref, wicl_ref, w0_ref, a0_ref, kk_ref, ka_ref, rk_ref, s0_ref,
                y_ref, bonus_ref, sfin_ref, state_ref, *, reverse, n_chunks, n_pp):
    jt = pl.program_id(2)
    c = CHUNK

    @pl.when(jt == 0)
    def _():
        state_ref[...] = s0_ref[0]

    zs = zs_ref[...]
    lane = lax.broadcasted_iota(jnp.int32, (PAIR, PAIR), 1)
    row = lax.broadcasted_iota(jnp.int32, (PAIR, PAIR), 0)
    head_ones = ((lane // HEAD_DIM) == (row // HEAD_DIM)).astype(F32)
    if reverse:
        strict, incl = lane > row, lane >= row
    else:
        strict, incl = row > lane, row >= lane
    tri = incl[:c, :c].astype(F32)
    head0 = lax.broadcasted_iota(jnp.int32, (c, PAIR), 1) < HEAD_DIM

    w_lin = w0_ref[0] + _dot(jnp.tanh(zs), wdec_ref[0])
    softplus = jnp.maximum(-w_lin, 0.0) + jnp.log(1.0 + jnp.exp(-jnp.abs(w_lin)))
    log_decay = -jnp.exp(-softplus - 0.5)
    a_lr = _sigmoid(a0_ref[0] + _dot(zs, wicl_ref[0]))
    r = r_ref[...]
    k = k_ref[...]
    v = v_ref[...]
    kk = k * kk_ref[...]
    k_d = k * (1.0 + (a_lr - 1.0) * ka_ref[...])
    kk_sq = kk * kk
    rkr = r * k_d * rk_ref[...]
    lanes = lambda x, pi: x[:, pi * PAIR:(pi + 1) * PAIR]
    kk_norm = jnp.concatenate([jnp.sqrt(_dot_split(lanes(kk_sq, pi), head_ones)) for pi in range(n_pp)], axis=1)
    bonus_ref[0] = jnp.concatenate([_dot_split(lanes(rkr, pi), head_ones) for pi in range(n_pp)], axis=1) * v
    kk = kk / jnp.maximum(kk_norm, L2_EPS)
    a_vec = -kk
    b_vec = kk * a_lr

    def stack(x):
        return jnp.concatenate([jnp.where(head0, x, 0.0), jnp.where(head0, 0.0, x)], axis=0)

    units = [(pi, ci) for ci in range(n_chunks) for pi in range(n_pp)]
    cut = lambda x, u: x[u[1] * c:(u[1] + 1) * c, u[0] * PAIR:(u[0] + 1) * PAIR]
    lw = [cut(log_decay, u) for u in units]
    cum = [_dot_split_rhs(tri, x) for x in lw]
    tot = [x[0:1] if reverse else x[c - 1:c] for x in cum]
    g_in = [jnp.exp(x) for x in cum]
    g_ex = [jnp.exp(x - y) for x, y in zip(cum, lw)]
    g_inv = [jnp.exp(-x) for x in cum]
    g_hat = [jnp.exp(t - x) for t, x in zip(tot, cum)]
    g_tot = [jnp.exp(t) for t in tot]
    aa = [stack(cut(a_vec, u) * g) for u, g in zip(units, g_ex)]
    ra = [stack(cut(r, u) * g) for u, g in zip(units, g_in)]
    bb = [stack(cut(b_vec, u) * g) for u, g in zip(units, g_inv)]
    kt = [stack(cut(k_d, u) * g) for u, g in zip(units, g_inv)]
    bh = [stack(cut(b_vec, u) * g) for u, g in zip(units, g_hat)]
    kh = [stack(cut(k_d, u) * g) for u, g in zip(units, g_hat)]
    vm = [stack(cut(v, u)) for u in units]
    scores = [_dot_nt(jnp.concatenate([a, b], axis=0), jnp.concatenate([e, f], axis=0))
              for a, b, e, f in zip(aa, ra, bb, kt)]
    n_pow = [jnp.where(strict, s[:PAIR, :PAIR], 0.0) for s in scores]
    n_ak = [jnp.where(strict, s[:PAIR, PAIR:], 0.0) for s in scores]
    m_rb = [jnp.where(incl, s[PAIR:, :PAIR], 0.0) for s in scores]
    m_rk = [jnp.where(incl, s[PAIR:, PAIR:], 0.0) for s in scores]
    xs = [jnp.concatenate([a, _dot(n, w)], axis=1) for a, n, w in zip(aa, n_ak, vm)]
    steps = CHUNK.bit_length() - 1
    for si in range(steps):
        xs = [x + _dot(n, x) for n, x in zip(n_pow, xs)]
        if si + 1 < steps:
            n_pow = [_dot(n, n) for n in n_pow]
    ah = [x[:, :PAIR] for x in xs]
    ul = [x[:, PAIR:] for x in xs]
    pp = [_dot_tn(a, b) for a, b in zip(ah, bh)]
    qq = [_dot_tn(jnp.concatenate([u_, w], axis=0), jnp.concatenate([b, k_], axis=0))
          for u_, w, b, k_ in zip(ul, vm, bh, kh)]
    rh = [x + _dot(m, a) for x, m, a in zip(ra, m_rb, ah)]
    y_loc = [_dot(m, u_) + _dot(n, w) for m, u_, n, w in zip(m_rb, ul, m_rk, vm)]

    state = [state_ref[pi] for pi in range(n_pp)]
    ys = {}
    for ci in (range(n_chunks - 1, -1, -1) if reverse else range(n_chunks)):
        for pi in range(n_pp):
            ui = units.index((pi, ci))
            ym = y_loc[ui] + _dot_nt(rh[ui], state[pi])
            state[pi] = state[pi] * g_tot[ui] + _dot(state[pi], pp[ui]) + qq[ui]
            ys[(pi, ci)] = ym[:c] + ym[c:]
    y_ref[0] = jnp.concatenate(
        [jnp.concatenate([ys[(pi, ci)] for pi in range(n_pp)], axis=1) for ci in range(n_chunks)], axis=0)
    for pi in range(n_pp):
        state_ref[pi] = state[pi]

    @pl.when(jt == pl.num_programs(2) - 1)
    def _():
        for pi in range(n_pp):
            sfin_ref[0, pi] = state[pi]


def _wkv(z, s0p, wdec, wicl, w0, a0, k_k, k_a, r_k, *, n_seq, seq_len, tb, reverse, col_r, col_k, col_v, col_s):
    t = z.shape[0]
    n_pairs = k_k.shape[1] // PAIR
    nb = seq_len // tb
    d = 1 if reverse else 0
    n_pp = WKV_PAIRS_PER_STEP
    wl = n_pp * PAIR

    def tblk(b, j):
        return b * nb + ((nb - 1 - j) if reverse else j)

    kern = functools.partial(_wkv_kernel, reverse=reverse, n_chunks=tb // CHUNK, n_pp=n_pp)
    tok = lambda col: pl.BlockSpec((tb, wl), lambda b, p, j: (tblk(b, j), col // wl + p))
    vec = pl.BlockSpec((1, wl), lambda b, p, j: (0, p))
    dvec = pl.BlockSpec((1, 1, wl), lambda b, p, j: (d, 0, p))
    dmat = pl.BlockSpec((1, R_SMALL, wl), lambda b, p, j: (d, 0, p))
    st = pl.BlockSpec((1, n_pp, PAIR, PAIR), lambda b, p, j: (b, p, 0, 0))
    out_tok = pl.BlockSpec((1, tb, wl), lambda b, p, j: (0, tblk(b, j), p))
    y, bonus, sfin = pl.pallas_call(
        kern,
        grid=(n_seq, n_pairs // n_pp, nb),
        in_specs=[tok(col_r), tok(col_k), tok(col_v),
                  pl.BlockSpec((tb, R_SMALL), lambda b, p, j: (tblk(b, j), col_s // R_SMALL)),
                  dmat, dmat, dvec, dvec, vec, vec, vec, st],
        out_specs=[out_tok, out_tok, st],
        out_shape=[jax.ShapeDtypeStruct((1, t, n_pairs * PAIR), F32),
                   jax.ShapeDtypeStruct((1, t, n_pairs * PAIR), F32),
                   jax.ShapeDtypeStruct((n_seq, n_pairs, PAIR, PAIR), F32)],
        scratch_shapes=[pltpu.VMEM((n_pp, PAIR, PAIR), F32)],
        compiler_params=_cparams(),
        name="wkv_bwd" if reverse else "wkv_fwd",
    )(z, z, z, z, wdec, wicl, w0, a0, k_k, k_a, r_k, s0p)
    return y[0], bonus[0], sfin


def _shift_prev(x, period):
    t = lax.broadcasted_iota(jnp.int32, x.shape, 0)
    return jnp.where((t & (period - 1)) == 0, 0.0, pltpu.roll(x, 1, 0))


def _shift_next(x, period):
    t = lax.broadcasted_iota(jnp.int32, x.shape, 0)
    return jnp.where((t & (period - 1)) == period - 1, 0.0, pltpu.roll(x, x.shape[0] - 1, 0))


def _mix_kernel(x_ref, yf_ref, yb_ref, bf_ref, bb_ref, gd_ref, cb_ref, cc_ref, cx_ref, ga_ref, gb_ref, mod_ref,
                gnw_ref, gnb_ref, g2_ref, cw_ref, cbias_ref, wpa_ref, wpb_ref, wo_ref, o_ref,
                *, tm, seq_len, row0, period):
    i = pl.program_id(0)
    d = x_ref.shape[1]
    lane = lax.broadcasted_iota(jnp.int32, (PAIR, PAIR), 1)
    row = lax.broadcasted_iota(jnp.int32, (PAIR, PAIR), 0)
    head_mean = jnp.where((lane // HEAD_DIM) == (row // HEAD_DIM), 1.0 / HEAD_DIM, 0.0).astype(F32)

    y = yf_ref[...] + yb_ref[...]
    parts = []
    for s in range(d // PAIR):
        ys = y[:, s * PAIR:(s + 1) * PAIR]
        mu = _dot_split(ys, head_mean)
        yc = ys - mu
        var = _dot_split(yc * yc, head_mean)
        parts.append(yc * lax.rsqrt(var + GN_EPS))
    yn = jnp.concatenate(parts, axis=1)
    o = yn * gnw_ref[...] + gnb_ref[...] + bf_ref[...] + bb_ref[...]
    gate = _dot(_sigmoid(gd_ref[...]), g2_ref[...])
    o_a = o * gate

    p = cc_ref[...] * cx_ref[...]
    cw = cw_ref[...]
    conv = _shift_prev(p, period) * cw[0:1] + p * cw[1:2] + _shift_next(p, period) * cw[2:3] + cbias_ref[...]
    o_b = cb_ref[...] * conv

    merged = _sigmoid(ga_ref[...]) * _dot(o_a, wpa_ref[...]) + _sigmoid(gb_ref[...]) * _dot(o_b, wpb_ref[...])
    mrow = _mod_row(i, tm, seq_len, row0)
    gt1 = mod_ref[pl.ds(mrow, 1), 2 * d:3 * d]
    o_ref[...] = x_ref[...] + gt1 * _dot(merged, wo_ref[...])


def _mix(x, yf, yb, bonf, bonb, z, mod_l, gn_w, gn_b, g2, conv_w, conv_b, w_pa, w_pb, w_o,
         *, tm, seq_len, row0, period, cols):
    t, d = x.shape
    kern = functools.partial(_mix_kernel, tm=tm, seq_len=seq_len, row0=row0, period=period)
    tok = pl.BlockSpec((tm, d), lambda i: (i, 0))
    zcol = lambda col: pl.BlockSpec((tm, d), lambda i: (i, col // d))
    full = lambda a: pl.BlockSpec(a.shape, lambda i: (0,) * a.ndim)
    vec = pl.BlockSpec((1, d), lambda i: (0, 0))
    return pl.pallas_call(
        kern,
        grid=(t // tm,),
        in_specs=[tok, tok, tok, tok, tok,
                  pl.BlockSpec((tm, R_G), lambda i: (i, cols["gd"] // R_G)),
                  zcol(cols["cb"]), zcol(cols["cc"]), zcol(cols["cx"]), zcol(cols["ga"]), zcol(cols["gb"]),
                  full(mod_l), vec, vec, full(g2), full(conv_w), vec, full(w_pa), full(w_pb), full(w_o)],
        out_specs=tok,
        out_shape=jax.ShapeDtypeStruct((t, d), F32),
        compiler_params=_cparams(),
        name="mix",
    )(x, yf, yb, bonf, bonb, z, z, z, z, z, z, mod_l, gn_w.reshape(1, d), gn_b.reshape(1, d), g2, conv_w,
      conv_b.reshape(1, d), w_pa, w_pb, w_o)


def _ffn_kernel(x_ref, u_ref, up_ref, un_ref, mod_ref, cw_ref, cb_ref, wd_ref, gf_ref, o_ref,
                *, tm, seq_len, row0, grid, final, kc):
    i = pl.program_id(0)
    d = x_ref.shape[1]
    d_ff = wd_ref.shape[0]
    tiles_per_seq = seq_len // tm
    first = (i % tiles_per_seq) == 0
    last = (i % tiles_per_seq) == tiles_per_seq - 1

    def conv(col):
        u = u_ref[:, col:col + kc]
        if grid:
            halo_p = jnp.where(first, 0.0, up_ref[:, col:col + kc])
            halo_n = jnp.where(last, 0.0, un_ref[:, col:col + kc])
            prev = jnp.concatenate([halo_p, u[:tm - GRID_W]], axis=0)
            nxt = jnp.concatenate([u[GRID_W:], halo_n], axis=0)
        else:
            prev = _shift_prev(u, seq_len)
            nxt = _shift_next(u, seq_len)
        cw = cw_ref[:, col:col + kc]
        return prev * cw[0:1] + u * cw[1:2] + nxt * cw[2:3] + cb_ref[:, col:col + kc]

    acc = jnp.zeros((tm, d), F32)
    for ci in range(d_ff // kc):
        act = conv(ci * kc)
        lin = conv(d_ff + ci * kc)
        acc = acc + _dot(_silu(act) * lin, wd_ref[ci * kc:(ci + 1) * kc, :])
    mrow = _mod_row(i, tm, seq_len, row0)
    gt2 = mod_ref[pl.ds(mrow, 1), 5 * d:6 * d]
    xn = x_ref[...] + gt2 * acc
    if final:
        xn = xn * lax.rsqrt(jnp.mean(xn * xn, axis=-1, keepdims=True) + NORM_EPS) * gf_ref[...]
    o_ref[...] = xn


def _ffn(x, u, mod_l, conv_w, conv_b, w_down, norm_f_g, *, tm, seq_len, row0, grid, final):
    t, d = x.shape
    n_up = u.shape[1]
    hb = tm // GRID_W
    n_halo = t // GRID_W
    kern = functools.partial(_ffn_kernel, tm=tm, seq_len=seq_len, row0=row0, grid=grid, final=final, kc=256)
    full = lambda a: pl.BlockSpec(a.shape, lambda i: (0,) * a.ndim)
    return pl.pallas_call(
        kern,
        grid=(t // tm,),
        in_specs=[pl.BlockSpec((tm, d), lambda i: (i, 0)),
                  pl.BlockSpec((tm, n_up), lambda i: (i, 0)),
                  pl.BlockSpec((GRID_W, n_up), lambda i: (jnp.maximum(i * hb - 1, 0), 0)),
                  pl.BlockSpec((GRID_W, n_up), lambda i: (jnp.minimum(i * hb + hb, n_halo - 1), 0)),
                  full(mod_l), full(conv_w), pl.BlockSpec((1, n_up), lambda i: (0, 0)), full(w_down),
                  pl.BlockSpec((1, d), lambda i: (0, 0))],
        out_specs=pl.BlockSpec((tm, d), lambda i: (i, 0)),
        out_shape=jax.ShapeDtypeStruct((t, d), F32),
        compiler_params=_cparams(),
        name="ffn",
    )(x, u, u, u, mod_l, conv_w, conv_b.reshape(1, n_up), w_down, norm_f_g.reshape(1, d))


def _pack_state(s):
    b, h, n, _ = s.shape
    eye = jnp.eye(2, dtype=s.dtype)
    return jnp.einsum("bphij,hg->bphigj", s.reshape(b, h // 2, 2, n, n), eye).reshape(b, h // 2, 2 * n, 2 * n)


def _unpack_state(sp):
    b, p, n2, _ = sp.shape
    n = n2 // 2
    eye = jnp.eye(2, dtype=sp.dtype)
    return jnp.einsum("bphigj,hg->bphij", sp.reshape(b, p, 2, n, 2, n), eye).reshape(b, 2 * p, n, n)


def kernel(x_prompt, x_sample, state_wkv, c, c_ctx, w_mod, b_mod, norm1_g, w_in, decay_w0, decay_w2, iclr_a0,
           iclr_a2, gate_g2, k_k, k_a, r_k, gn_w, gn_b, conv_mix_w, conv_mix_b, w_pa, w_pb, w_o, norm2_g, w_up,
           conv_ffn_w, conv_ffn_b, w_down, norm_f_g):
    n_ctx, ctx_len, d = x_prompt.shape
    n_lat, lat_len, _ = x_sample.shape
    n_layers = w_mod.shape[0]
    d_a = k_k.shape[1]
    n_heads = d_a // HEAD_DIM
    r_w = decay_w2.shape[2]
    assert d_a == d and 4 * r_w == R_SMALL and gate_g2.shape[1] == R_G

    o_small = 3 * d_a
    o_gd = o_small + R_SMALL
    o_cb = o_gd + R_G
    o_gates = o_cb + 3 * d
    cols = dict(r=0, k=d, v=2 * d, cb=3 * d, cc=4 * d, cx=5 * d, ga=6 * d, gb=7 * d, small=8 * d,
                gd=8 * d + R_SMALL)
    n_in = 8 * d + R_SMALL + R_G
    n_in_pad = 8 * d + 512
    w_in_p = jnp.concatenate(
        [w_in[:, :, :o_small], w_in[:, :, o_cb:o_gates], w_in[:, :, o_gates:], w_in[:, :, o_small:o_cb],
         jnp.zeros((n_layers, d, n_in_pad - n_in), w_in.dtype)], axis=2).astype(BF16)
    w_up_b = w_up.astype(BF16)
    w_pa_b, w_pb_b, w_o_b = w_pa.astype(BF16), w_pb.astype(BF16), w_o.astype(BF16)
    w_down_b = w_down.astype(BF16)
    g2_b = gate_g2.astype(BF16)

    zrow = jnp.zeros((n_layers, r_w, d_a), F32)
    wdec = jnp.stack([jnp.concatenate([decay_w2[:, 0], zrow, zrow, zrow], axis=1),
                      jnp.concatenate([zrow, decay_w2[:, 1], zrow, zrow], axis=1)], axis=1)
    wicl = jnp.stack([jnp.concatenate([zrow, zrow, iclr_a2[:, 0], zrow], axis=1),
                      jnp.concatenate([zrow, zrow, zrow, iclr_a2[:, 1]], axis=1)], axis=1)

    cond8 = jnp.concatenate([c_ctx[None, :], c, jnp.zeros((8 - 1 - n_lat, d), F32)], axis=0)
    mod = _modulation(cond8, w_mod, b_mod)

    def layer(x, l, *, n_seq, seq_len, row0, grid, s0, final):
        tm = 256
        z = _norm_mm(x, norm1_g[l], mod[l], w_in_p[l], tm=512, tn=n_in_pad // 4, seq_len=seq_len, row0=row0,
                     sc_col=1, sh_col=0, name="in_proj")
        outs = []
        for rev in (False, True):
            dd = 1 if rev else 0
            outs.append(_wkv(z, s0[dd], wdec[l], wicl[l], decay_w0[l].reshape(2, 1, d_a),
                             iclr_a0[l].reshape(2, 1, d_a), k_k[l].reshape(1, d_a), k_a[l].reshape(1, d_a),
                             r_k[l].reshape(1, d_a), n_seq=n_seq, seq_len=seq_len, tb=256, reverse=rev,
                             col_r=cols["r"], col_k=cols["k"], col_v=cols["v"], col_s=cols["small"]))
        (yf, bonf, sf), (yb, bonb, sb) = outs
        x = _mix(x, yf, yb, bonf, bonb, z, mod[l], gn_w[l], gn_b[l], g2_b[l], conv_mix_w[l], conv_mix_b[l],
                 w_pa_b[l], w_pb_b[l], w_o_b[l], tm=tm, seq_len=seq_len, row0=row0,
                 period=GRID_W if grid else seq_len, cols=cols)
        u = _norm_mm(x, norm2_g[l], mod[l], w_up_b[l], tm=512, tn=w_up.shape[2] // 4, seq_len=seq_len, row0=row0,
                     sc_col=4, sh_col=3, name="ffn_up")
        x = _ffn(x, u, mod[l], conv_ffn_w[l], conv_ffn_b[l], w_down_b[l], norm_f_g, tm=tm, seq_len=seq_len,
                 row0=row0, grid=grid, final=final)
        return x, sf, sb

    xp = x_prompt.reshape(n_ctx * ctx_len, d)
    zero_state = jnp.zeros((n_ctx, n_heads // 2, PAIR, PAIR), F32)
    ctx_states = []
    for l in range(n_layers):
        xp, sf, sb = layer(xp, l, n_seq=n_ctx, seq_len=ctx_len, row0=0, grid=False,
                           s0=(zero_state, zero_state), final=(l == n_layers - 1))
        ctx_states.append(jnp.stack([_unpack_state(sf), _unpack_state(sb)], axis=1))
    new_state = jnp.stack(ctx_states, axis=1).astype(x_prompt.dtype)
    y_prompt = xp.reshape(n_ctx, ctx_len, d)

    xs = x_sample.reshape(n_lat * lat_len, d)
    for l in range(n_layers):
        s0 = (_pack_state(state_wkv[:, l, 0].astype(F32)), _pack_state(state_wkv[:, l, 1].astype(F32)))
        xs, _, _ = layer(xs, l, n_seq=n_lat, seq_len=lat_len, row0=1, grid=True, s0=s0,
                         final=(l == n_layers - 1))
    y_sample = xs.reshape(n_lat, lat_len, d)
    return (y_prompt, y_sample, new_state)
```

```python
import functools

import jax
import jax.numpy as jnp
from jax import lax
from jax.experimental import pallas as pl
from jax.experimental.pallas import tpu as pltpu

F32 = jnp.float32
BF16 = jnp.bfloat16
HIGHEST = lax.Precision.HIGHEST

HEAD_DIM = 64
PAIR = 2 * HEAD_DIM
CHUNK = 64
WKV_PAIRS_PER_STEP = 2
GRID_W = 64
R_SMALL = 256
R_G = 128
NORM_EPS = 1e-6
GN_EPS = 64e-5
L2_EPS = 1e-12
VMEM_LIMIT = 56 * 1024 * 1024


def _cparams():
    return pltpu.CompilerParams(vmem_limit_bytes=VMEM_LIMIT)


def _dot(a, b, hi=False):
    if hi:
        return jnp.dot(a, b, precision=HIGHEST, preferred_element_type=F32)
    return jnp.dot(a.astype(BF16), b.astype(BF16), preferred_element_type=F32)


def _dot_nt(a, b):
    return lax.dot_general(a.astype(BF16), b.astype(BF16), (((1,), (1,)), ((), ())),
                           preferred_element_type=F32)


def _dot_tn(a, b):
    return jnp.dot(a.T.astype(BF16), b.astype(BF16), preferred_element_type=F32)


def _split(x):
    hi = x.astype(BF16)
    return hi, (x - hi.astype(F32)).astype(BF16)


def _dot_split(a, b):
    hi, lo = _split(a)
    bb = b.astype(BF16)
    return jnp.dot(hi, bb, preferred_element_type=F32) + jnp.dot(lo, bb, preferred_element_type=F32)


def _dot_split_rhs(a, b):
    hi, lo = _split(b)
    ab = a.astype(BF16)
    return jnp.dot(ab, hi, preferred_element_type=F32) + jnp.dot(ab, lo, preferred_element_type=F32)


def _sigmoid(x):
    return 1.0 / (1.0 + jnp.exp(-x))


def _silu(x):
    return x * _sigmoid(x)


def _mod_kernel(c_ref, w_ref, b_ref, o_ref):
    s = _silu(c_ref[...])
    o_ref[0] = _dot(s, w_ref[0], hi=True) + b_ref[0]


def _modulation(cond8, w_mod, b_mod):
    n_layers, d, n = w_mod.shape
    tn = n // 4
    return pl.pallas_call(
        _mod_kernel,
        grid=(n_layers, n // tn),
        in_specs=[pl.BlockSpec((8, d), lambda l, j: (0, 0)),
                  pl.BlockSpec((1, d, tn), lambda l, j: (l, 0, j)),
                  pl.BlockSpec((1, 1, tn), lambda l, j: (l, 0, j))],
        out_specs=pl.BlockSpec((1, 8, tn), lambda l, j: (l, 0, j)),
        out_shape=jax.ShapeDtypeStruct((n_layers, 8, n), F32),
        compiler_params=_cparams(),
        name="modulation",
    )(cond8, w_mod, b_mod.reshape(n_layers, 1, n))


def _mod_row(i, tm, seq_len, row0):
    if row0 == 0:
        return 0
    return row0 + (i * tm) // seq_len


def _norm_mm_kernel(x_ref, g_ref, mod_ref, w_ref, o_ref, h_ref, *, tm, seq_len, row0, sc_col, sh_col):
    i = pl.program_id(0)
    d = x_ref.shape[1]

    @pl.when(pl.program_id(1) == 0)
    def _():
        x = x_ref[...]
        y = x * lax.rsqrt(jnp.mean(x * x, axis=-1, keepdims=True) + NORM_EPS) * g_ref[...]
        row = _mod_row(i, tm, seq_len, row0)
        sc = mod_ref[pl.ds(row, 1), sc_col * d:(sc_col + 1) * d]
        sh = mod_ref[pl.ds(row, 1), sh_col * d:(sh_col + 1) * d]
        h_ref[...] = (y * (1.0 + sc) + sh).astype(BF16)

    o_ref[...] = jnp.dot(h_ref[...], w_ref[...], preferred_element_type=F32)


def _norm_mm(x, g, mod_l, w, *, tm, tn, seq_len, row0, sc_col, sh_col, name):
    t, d = x.shape
    n = w.shape[1]
    kern = functools.partial(_norm_mm_kernel, tm=tm, seq_len=seq_len, row0=row0, sc_col=sc_col, sh_col=sh_col)
    return pl.pallas_call(
        kern,
        grid=(t // tm, n // tn),
        in_specs=[pl.BlockSpec((tm, d), lambda i, j: (i, 0)),
                  pl.BlockSpec((1, d), lambda i, j: (0, 0)),
                  pl.BlockSpec(mod_l.shape, lambda i, j: (0, 0)),
                  pl.BlockSpec((d, tn), lambda i, j: (0, j))],
        out_specs=pl.BlockSpec((tm, tn), lambda i, j: (i, j)),
        out_shape=jax.ShapeDtypeStruct((t, n), F32),
        scratch_shapes=[pltpu.VMEM((tm, d), BF16)],
        compiler_params=_cparams(),
        name=name,
    )(x, g.reshape(1, d), mod_l, w)


def _wkv_kernel(r_ref, k_ref, v_ref, zs_ref, wdec_ref, wicl_ref, w0_ref, a0_ref, kk_ref, ka_ref, rk_ref, s0_ref,
                y_ref, bonus_ref, sfin_ref, state_ref, *, reverse, n_chunks, n_pp):
    jt = pl.program_id(2)
    c = CHUNK

    @pl.when(jt == 0)
    def _():
        state_ref[...] = jnp.zeros(state_ref.shape, F32)
        for hi in range(2 * n_pp):
            lo = (hi % 2) * HEAD_DIM
            state_ref[hi // 2, lo:lo + HEAD_DIM, lo:lo + HEAD_DIM] = s0_ref[0, hi]

    zs = zs_ref[...]
    lane = lax.broadcasted_iota(jnp.int32, (PAIR, PAIR), 1)
    row = lax.broadcasted_iota(jnp.int32, (PAIR, PAIR), 0)
    own = (lane // HEAD_DIM) == (row // HEAD_DIM)
    head0_row = lax.broadcasted_iota(jnp.int32, (1, PAIR), 1) < HEAD_DIM

    def head_sum(x):
        s0 = jnp.sum(jnp.where(head0_row, x, 0.0), axis=-1, keepdims=True)
        s1 = jnp.sum(jnp.where(head0_row, 0.0, x), axis=-1, keepdims=True)
        return jnp.where(head0_row, s0, s1)

    def swap_heads(x):
        return pltpu.roll(x, HEAD_DIM, 1)

    if reverse:
        strict, incl = lane > row, lane >= row
    else:
        strict, incl = row > lane, row >= lane
    tri = incl[:c, :c].astype(F32)
    head0 = lax.broadcasted_iota(jnp.int32, (c, PAIR), 1) < HEAD_DIM

    w_lin = w0_ref[0] + _dot(jnp.tanh(zs), wdec_ref[0])
    softplus = jnp.maximum(-w_lin, 0.0) + jnp.log(1.0 + jnp.exp(-jnp.abs(w_lin)))
    log_decay = -jnp.exp(-softplus - 0.5)
    a_lr = _sigmoid(a0_ref[0] + _dot(zs, wicl_ref[0]))
    r = r_ref[...]
    k = k_ref[...]
    v = v_ref[...]
    kk = k * kk_ref[...]
    k_d = k * (1.0 + (a_lr - 1.0) * ka_ref[...])
    kk_sq = kk * kk
    rkr = r * k_d * rk_ref[...]
    lanes = lambda x, pi: x[:, pi * PAIR:(pi + 1) * PAIR]
    kk_norm = jnp.concatenate([jnp.sqrt(head_sum(lanes(kk_sq, pi))) for pi in range(n_pp)], axis=1)
    bonus_ref[0] = jnp.concatenate([head_sum(lanes(rkr, pi)) for pi in range(n_pp)], axis=1) * v
    kk = kk / jnp.maximum(kk_norm, L2_EPS)
    a_vec = -kk
    b_vec = kk * a_lr

    def stack(x):
        return jnp.concatenate([jnp.where(head0, x, 0.0), jnp.where(head0, 0.0, x)], axis=0)

    units = [(pi, ci) for ci in range(n_chunks) for pi in range(n_pp)]
    cut = lambda x, u: x[u[1] * c:(u[1] + 1) * c, u[0] * PAIR:(u[0] + 1) * PAIR]
    lw = [cut(log_decay, u) for u in units]
    cum = [_dot_split_rhs(tri, x) for x in lw]
    tot = [x[0:1] if reverse else x[c - 1:c] for x in cum]
    g_in = [jnp.exp(x) for x in cum]
    g_ex = [jnp.exp(x - y) for x, y in zip(cum, lw)]
    g_inv = [jnp.exp(-x) for x in cum]
    g_hat = [jnp.exp(t - x) for t, x in zip(tot, cum)]
    g_tot = [jnp.exp(t) for t in tot]
    aa = [stack(cut(a_vec, u) * g) for u, g in zip(units, g_ex)]
    ra = [stack(cut(r, u) * g) for u, g in zip(units, g_in)]
    bb = [stack(cut(b_vec, u) * g) for u, g in zip(units, g_inv)]
    kt = [stack(cut(k_d, u) * g) for u, g in zip(units, g_inv)]
    bh = [stack(cut(b_vec, u) * g) for u, g in zip(units, g_hat)]
    kh = [stack(cut(k_d, u) * g) for u, g in zip(units, g_hat)]
    vm = [stack(cut(v, u)) for u in units]
    scores = [_dot_nt(jnp.concatenate([a, b], axis=0), jnp.concatenate([e, f], axis=0))
              for a, b, e, f in zip(aa, ra, bb, kt)]
    n_pow = [jnp.where(strict, s[:PAIR, :PAIR], 0.0) for s in scores]
    n_ak = [jnp.where(strict, s[:PAIR, PAIR:], 0.0) for s in scores]
    m_rb = [jnp.where(incl, s[PAIR:, :PAIR], 0.0) for s in scores]
    m_rk = [jnp.where(incl, s[PAIR:, PAIR:], 0.0) for s in scores]
    xs = [a + swap_heads(_dot(n, w)) for a, n, w in zip(aa, n_ak, vm)]
    steps = CHUNK.bit_length() - 1
    for si in range(steps):
        if si + 1 < steps:
            prod = [_dot(n, jnp.concatenate([x, n], axis=1)) for n, x in zip(n_pow, xs)]
            xs = [x + p[:, :PAIR] for x, p in zip(xs, prod)]
            n_pow = [p[:, PAIR:] for p in prod]
        else:
            xs = [x + _dot(n, x) for n, x in zip(n_pow, xs)]
    vsw = [swap_heads(w) for w in vm]
    oc = [_dot(jnp.concatenate([mb, mk], axis=1), jnp.concatenate([x, w], axis=0))
          for mb, mk, x, w in zip(m_rb, m_rk, xs, vsw)]
    rh = [x + jnp.where(own, o, 0.0) for x, o in zip(ra, oc)]
    y_loc = [swap_heads(jnp.where(own, 0.0, o)) for o in oc]
    rq = [_dot_tn(jnp.concatenate([x, w], axis=0), jnp.concatenate([b, k_], axis=0))
          for x, w, b, k_ in zip(xs, vsw, bh, kh)]
    pp = [jnp.where(own, q, 0.0) for q in rq]
    qq = [jnp.where(own, jnp.concatenate([q[HEAD_DIM:], q[:HEAD_DIM]], axis=0), 0.0) for q in rq]

    state = [state_ref[pi] for pi in range(n_pp)]
    ys = {}
    for ci in (range(n_chunks - 1, -1, -1) if reverse else range(n_chunks)):
        for pi in range(n_pp):
            ui = units.index((pi, ci))
            ym = y_loc[ui] + _dot_nt(rh[ui], state[pi])
            state[pi] = state[pi] * g_tot[ui] + _dot(state[pi], pp[ui]) + qq[ui]
            ys[(pi, ci)] = ym[:c] + ym[c:]
    y_ref[0] = jnp.concatenate(
        [jnp.concatenate([ys[(pi, ci)] for pi in range(n_pp)], axis=1) for ci in range(n_chunks)], axis=0)
    for pi in range(n_pp):
        state_ref[pi] = state[pi]

    @pl.when(jt == pl.num_programs(2) - 1)
    def _():
        for hi in range(2 * n_pp):
            lo = (hi % 2) * HEAD_DIM
            sfin_ref[0, hi] = state[hi // 2][lo:lo + HEAD_DIM, lo:lo + HEAD_DIM]


def _wkv(z, s0p, wdec, wicl, w0, a0, k_k, k_a, r_k, *, n_seq, seq_len, tb, reverse, col_r, col_k, col_v, col_s):
    t = z.shape[0]
    n_pairs = k_k.shape[1] // PAIR
    nb = seq_len // tb
    d = 1 if reverse else 0
    n_pp = WKV_PAIRS_PER_STEP
    wl = n_pp * PAIR

    def tblk(b, j):
        return b * nb + ((nb - 1 - j) if reverse else j)

    kern = functools.partial(_wkv_kernel, reverse=reverse, n_chunks=tb // CHUNK, n_pp=n_pp)
    tok = lambda col: pl.BlockSpec((tb, wl), lambda b, p, j: (tblk(b, j), col // wl + p))
    vec = pl.BlockSpec((1, wl), lambda b, p, j: (0, p))
    dvec = pl.BlockSpec((1, 1, wl), lambda b, p, j: (d, 0, p))
    dmat = pl.BlockSpec((1, R_SMALL, wl), lambda b, p, j: (d, 0, p))
    st = pl.BlockSpec((1, 2 * n_pp, HEAD_DIM, HEAD_DIM), lambda b, p, j: (b, p, 0, 0))
    out_tok = pl.BlockSpec((1, tb, wl), lambda b, p, j: (0, tblk(b, j), p))
    y, bonus, sfin = pl.pallas_call(
        kern,
        grid=(n_seq, n_pairs // n_pp, nb),
        in_specs=[tok(col_r), tok(col_k), tok(col_v),
                  pl.BlockSpec((tb, R_SMALL), lambda b, p, j: (tblk(b, j), col_s // R_SMALL)),
                  dmat, dmat, dvec, dvec, vec, vec, vec, st],
        out_specs=[out_tok, out_tok, st],
        out_shape=[jax.ShapeDtypeStruct((1, t, n_pairs * PAIR), F32),
                   jax.ShapeDtypeStruct((1, t, n_pairs * PAIR), F32),
                   jax.ShapeDtypeStruct((n_seq, 2 * n_pairs, HEAD_DIM, HEAD_DIM), F32)],
        scratch_shapes=[pltpu.VMEM((n_pp, PAIR, PAIR), F32)],
        compiler_params=_cparams(),
        name="wkv_bwd" if reverse else "wkv_fwd",
    )(z, z, z, z, wdec, wicl, w0, a0, k_k, k_a, r_k, s0p)
    return y[0], bonus[0], sfin


def _shift_prev(x, period):
    t = lax.broadcasted_iota(jnp.int32, x.shape, 0)
    return jnp.where((t & (period - 1)) == 0, 0.0, pltpu.roll(x, 1, 0))


def _shift_next(x, period):
    t = lax.broadcasted_iota(jnp.int32, x.shape, 0)
    return jnp.where((t & (period - 1)) == period - 1, 0.0, pltpu.roll(x, x.shape[0] - 1, 0))


def _mix_kernel(x_ref, yf_ref, yb_ref, bf_ref, bb_ref, gd_ref, cb_ref, cc_ref, cx_ref, ga_ref, gb_ref, mod_ref,
                gnw_ref, gnb_ref, g2_ref, cw_ref, cbias_ref, wpa_ref, wpb_ref, wo_ref, o_ref,
                *, tm, seq_len, row0, period):
    i = pl.program_id(0)
    d = x_ref.shape[1]
    lane = lax.broadcasted_iota(jnp.int32, (PAIR, PAIR), 1)
    row = lax.broadcasted_iota(jnp.int32, (PAIR, PAIR), 0)
    head_mean = jnp.where((lane // HEAD_DIM) == (row // HEAD_DIM), 1.0 / HEAD_DIM, 0.0).astype(F32)

    y = yf_ref[...] + yb_ref[...]
    parts = []
    for s in range(d // PAIR):
        ys = y[:, s * PAIR:(s + 1) * PAIR]
        mu = _dot_split(ys, head_mean)
        yc = ys - mu
        var = _dot_split(yc * yc, head_mean)
        parts.append(yc * lax.rsqrt(var + GN_EPS))
    yn = jnp.concatenate(parts, axis=1)
    o = yn * gnw_ref[...] + gnb_ref[...] + bf_ref[...] + bb_ref[...]
    gate = _dot(_sigmoid(gd_ref[...]), g2_ref[...])
    o_a = o * gate

    p = cc_ref[...] * cx_ref[...]
    cw = cw_ref[...]
    conv = _shift_prev(p, period) * cw[0:1] + p * cw[1:2] + _shift_next(p, period) * cw[2:3] + cbias_ref[...]
    o_b = cb_ref[...] * conv

    merged = _sigmoid(ga_ref[...]) * _dot(o_a, wpa_ref[...]) + _sigmoid(gb_ref[...]) * _dot(o_b, wpb_ref[...])
    mrow = _mod_row(i, tm, seq_len, row0)
    gt1 = mod_ref[pl.ds(mrow, 1), 2 * d:3 * d]
    o_ref[...] = x_ref[...] + gt1 * _dot(merged, wo_ref[...])


def _mix(x, yf, yb, bonf, bonb, z, mod_l, gn_w, gn_b, g2, conv_w, conv_b, w_pa, w_pb, w_o,
         *, tm, seq_len, row0, period, cols):
    t, d = x.shape
    kern = functools.partial(_mix_kernel, tm=tm, seq_len=seq_len, row0=row0, period=period)
    tok = pl.BlockSpec((tm, d), lambda i: (i, 0))
    zcol = lambda col: pl.BlockSpec((tm, d), lambda i: (i, col // d))
    full = lambda a: pl.BlockSpec(a.shape, lambda i: (0,) * a.ndim)
    vec = pl.BlockSpec((1, d), lambda i: (0, 0))
    return pl.pallas_call(
        kern,
        grid=(t // tm,),
        in_specs=[tok, tok, tok, tok, tok,
                  pl.BlockSpec((tm, R_G), lambda i: (i, cols["gd"] // R_G)),
                  zcol(cols["cb"]), zcol(cols["cc"]), zcol(cols["cx"]), zcol(cols["ga"]), zcol(cols["gb"]),
                  full(mod_l), vec, vec, full(g2), full(conv_w), vec, full(w_pa), full(w_pb), full(w_o)],
        out_specs=tok,
        out_shape=jax.ShapeDtypeStruct((t, d), F32),
        compiler_params=_cparams(),
        name="mix",
    )(x, yf, yb, bonf, bonb, z, z, z, z, z, z, mod_l, gn_w.reshape(1, d), gn_b.reshape(1, d), g2, conv_w,
      conv_b.reshape(1, d), w_pa, w_pb, w_o)


def _ffn_kernel(x_ref, u_ref, up_ref, un_ref, mod_ref, cw_ref, cb_ref, wd_ref, gf_ref, o_ref,
                *, tm, seq_len, row0, grid, final, kc):
    i = pl.program_id(0)
    d = x_ref.shape[1]
    d_ff = wd_ref.shape[0]
    tiles_per_seq = seq_len // tm
    first = (i % tiles_per_seq) == 0
    last = (i % tiles_per_seq) == tiles_per_seq - 1

    def conv(col):
        u = u_ref[:, col:col + kc]
        if grid:
            halo_p = jnp.where(first, 0.0, up_ref[:, col:col + kc])
            halo_n = jnp.where(last, 0.0, un_ref[:, col:col + kc])
            prev = jnp.concatenate([halo_p, u[:tm - GRID_W]], axis=0)
            nxt = jnp.concatenate([u[GRID_W:], halo_n], axis=0)
        else:
            prev = _shift_prev(u, seq_len)
            nxt = _shift_next(u, seq_len)
        cw = cw_ref[:, col:col + kc]
        return prev * cw[0:1] + u * cw[1:2] + nxt * cw[2:3] + cb_ref[:, col:col + kc]

    acc = jnp.zeros((tm, d), F32)
    for ci in range(d_ff // kc):
        act = conv(ci * kc)
        lin = conv(d_ff + ci * kc)
        acc = acc + _dot(_silu(act) * lin, wd_ref[ci * kc:(ci + 1) * kc, :])
    mrow = _mod_row(i, tm, seq_len, row0)
    gt2 = mod_ref[pl.ds(mrow, 1), 5 * d:6 * d]
    xn = x_ref[...] + gt2 * acc
    if final:
        xn = xn * lax.rsqrt(jnp.mean(xn * xn, axis=-1, keepdims=True) + NORM_EPS) * gf_ref[...]
    o_ref[...] = xn


def _ffn(x, u, mod_l, conv_w, conv_b, w_down, norm_f_g, *, tm, seq_len, row0, grid, final):
    t, d = x.shape
    n_up = u.shape[1]
    hb = tm // GRID_W
    n_halo = t // GRID_W
    kern = functools.partial(_ffn_kernel, tm=tm, seq_len=seq_len, row0=row0, grid=grid, final=final, kc=256)
    full = lambda a: pl.BlockSpec(a.shape, lambda i: (0,) * a.ndim)
    return pl.pallas_call(
        kern,
        grid=(t // tm,),
        in_specs=[pl.BlockSpec((tm, d), lambda i: (i, 0)),
                  pl.BlockSpec((tm, n_up), lambda i: (i, 0)),
                  pl.BlockSpec((GRID_W, n_up), lambda i: (jnp.maximum(i * hb - 1, 0), 0)),
                  pl.BlockSpec((GRID_W, n_up), lambda i: (jnp.minimum(i * hb + hb, n_halo - 1), 0)),
                  full(mod_l), full(conv_w), pl.BlockSpec((1, n_up), lambda i: (0, 0)), full(w_down),
                  pl.BlockSpec((1, d), lambda i: (0, 0))],
        out_specs=pl.BlockSpec((tm, d), lambda i: (i, 0)),
        out_shape=jax.ShapeDtypeStruct((t, d), F32),
        compiler_params=_cparams(),
        name="ffn",
    )(x, u, u, u, mod_l, conv_w, conv_b.reshape(1, n_up), w_down, norm_f_g.reshape(1, d))


def kernel(x_prompt, x_sample, state_wkv, c, c_ctx, w_mod, b_mod, norm1_g, w_in, decay_w0, decay_w2, iclr_a0,
           iclr_a2, gate_g2, k_k, k_a, r_k, gn_w, gn_b, conv_mix_w, conv_mix_b, w_pa, w_pb, w_o, norm2_g, w_up,
           conv_ffn_w, conv_ffn_b, w_down, norm_f_g):
    n_ctx, ctx_len, d = x_prompt.shape
    n_lat, lat_len, _ = x_sample.shape
    n_layers = w_mod.shape[0]
    d_a = k_k.shape[1]
    n_heads = d_a // HEAD_DIM
    r_w = decay_w2.shape[2]
    assert d_a == d and 4 * r_w == R_SMALL and gate_g2.shape[1] == R_G

    o_small = 3 * d_a
    o_gd = o_small + R_SMALL
    o_cb = o_gd + R_G
    o_gates = o_cb + 3 * d
    cols = dict(r=0, k=d, v=2 * d, cb=3 * d, cc=4 * d, cx=5 * d, ga=6 * d, gb=7 * d, small=8 * d,
                gd=8 * d + R_SMALL)
    n_in = 8 * d + R_SMALL + R_G
    n_in_pad = 8 * d + 512
    w_in_p = jnp.concatenate(
        [w_in[:, :, :o_small], w_in[:, :, o_cb:o_gates], w_in[:, :, o_gates:], w_in[:, :, o_small:o_cb],
         jnp.zeros((n_layers, d, n_in_pad - n_in), w_in.dtype)], axis=2).astype(BF16)
    w_up_b = w_up.astype(BF16)
    w_pa_b, w_pb_b, w_o_b = w_pa.astype(BF16), w_pb.astype(BF16), w_o.astype(BF16)
    w_down_b = w_down.astype(BF16)
    g2_b = gate_g2.astype(BF16)

    zrow = jnp.zeros((n_layers, r_w, d_a), F32)
    wdec = jnp.stack([jnp.concatenate([decay_w2[:, 0], zrow, zrow, zrow], axis=1),
                      jnp.concatenate([zrow, decay_w2[:, 1], zrow, zrow], axis=1)], axis=1)
    wicl = jnp.stack([jnp.concatenate([zrow, zrow, iclr_a2[:, 0], zrow], axis=1),
                      jnp.concatenate([zrow, zrow, zrow, iclr_a2[:, 1]], axis=1)], axis=1)

    cond8 = jnp.concatenate([c_ctx[None, :], c, jnp.zeros((8 - 1 - n_lat, d), F32)], axis=0)
    mod = _modulation(cond8, w_mod, b_mod)

    def layer(x, l, *, n_seq, seq_len, row0, grid, s0, final):
        tm = 256
        z = _norm_mm(x, norm1_g[l], mod[l], w_in_p[l], tm=512, tn=n_in_pad // 4, seq_len=seq_len, row0=row0,
                     sc_col=1, sh_col=0, name="in_proj")
        outs = []
        for rev in (False, True):
            dd = 1 if rev else 0
            outs.append(_wkv(z, s0[dd], wdec[l], wicl[l], decay_w0[l].reshape(2, 1, d_a),
                             iclr_a0[l].reshape(2, 1, d_a), k_k[l].reshape(1, d_a), k_a[l].reshape(1, d_a),
                             r_k[l].reshape(1, d_a), n_seq=n_seq, seq_len=seq_len, tb=256, reverse=rev,
                             col_r=cols["r"], col_k=cols["k"], col_v=cols["v"], col_s=cols["small"]))
        (yf, bonf, sf), (yb, bonb, sb) = outs
        x = _mix(x, yf, yb, bonf, bonb, z, mod[l], gn_w[l], gn_b[l], g2_b[l], conv_mix_w[l], conv_mix_b[l],
                 w_pa_b[l], w_pb_b[l], w_o_b[l], tm=tm, seq_len=seq_len, row0=row0,
                 period=GRID_W if grid else seq_len, cols=cols)
        u = _norm_mm(x, norm2_g[l], mod[l], w_up_b[l], tm=512, tn=w_up.shape[2] // 4, seq_len=seq_len, row0=row0,
                     sc_col=4, sh_col=3, name="ffn_up")
        x = _ffn(x, u, mod[l], conv_ffn_w[l], conv_ffn_b[l], w_down_b[l], norm_f_g, tm=tm, seq_len=seq_len,
                 row0=row0, grid=grid, final=final)
        return x, sf, sb

    xp = x_prompt.reshape(n_ctx * ctx_len, d)
    zero_state = jnp.zeros((n_ctx, n_heads, HEAD_DIM, HEAD_DIM), F32)
    ctx_states = []
    for l in range(n_layers):
        xp, sf, sb = layer(xp, l, n_seq=n_ctx, seq_len=ctx_len, row0=0, grid=False,
                           s0=(zero_state, zero_state), final=(l == n_layers - 1))
        ctx_states.append(jnp.stack([sf, sb], axis=1))
    new_state = jnp.stack(ctx_states, axis=1).astype(x_prompt.dtype)
    y_prompt = xp.reshape(n_ctx, ctx_len, d)

    xs = x_sample.reshape(n_lat * lat_len, d)
    for l in range(n_layers):
        s0 = (state_wkv[:, l, 0].astype(F32), state_wkv[:, l, 1].astype(F32))
        xs, _, _ = layer(xs, l, n_seq=n_lat, seq_len=lat_len, row0=1, grid=True, s0=s0,
                         final=(l == n_layers - 1))
    y_sample = xs.reshape(n_lat, lat_len, d)
    return (y_prompt, y_sample, new_state)
```

```python
import functools

import jax
import jax.numpy as jnp
from jax import lax
from jax.experimental import pallas as pl
from jax.experimental.pallas import tpu as pltpu

F32 = jnp.float32
BF16 = jnp.bfloat16
HIGHEST = lax.Precision.HIGHEST

HEAD_DIM = 64
PAIR = 2 * HEAD_DIM
CHUNK = 64
WKV_PAIRS_PER_STEP = 2
WKV_BLOCK = 256
TOKEN_TILE = 256
MM_TOKEN_TILE = 1024
FFN_COL_CHUNK = 256
GRID_W = 64
R_SMALL = 256
R_G = 128
NORM_EPS = 1e-6
GN_EPS = 64e-5
L2_EPS = 1e-12
VMEM_LIMIT = 56 * 1024 * 1024


def _cparams():
    return pltpu.CompilerParams(vmem_limit_bytes=VMEM_LIMIT)


def _dot(a, b, hi=False):
    if hi:
        return jnp.dot(a, b, precision=HIGHEST, preferred_element_type=F32)
    return jnp.dot(a.astype(BF16), b.astype(BF16), preferred_element_type=F32)


def _dot_nt(a, b):
    return lax.dot_general(a.astype(BF16), b.astype(BF16), (((1,), (1,)), ((), ())),
                           preferred_element_type=F32)


def _dot_tn(a, b):
    return jnp.dot(a.T.astype(BF16), b.astype(BF16), preferred_element_type=F32)


def _split(x):
    hi = x.astype(BF16)
    return hi, (x - hi.astype(F32)).astype(BF16)


def _dot_split(a, b):
    hi, lo = _split(a)
    bb = b.astype(BF16)
    return jnp.dot(hi, bb, preferred_element_type=F32) + jnp.dot(lo, bb, preferred_element_type=F32)


def _dot_split_rhs(a, b):
    hi, lo = _split(b)
    ab = a.astype(BF16)
    return jnp.dot(ab, hi, preferred_element_type=F32) + jnp.dot(ab, lo, preferred_element_type=F32)


def _sigmoid(x):
    return 1.0 / (1.0 + jnp.exp(-x))


def _silu(x):
    return x * _sigmoid(x)


def _mod_kernel(c_ref, w_ref, b_ref, o_ref):
    s = _silu(c_ref[...])
    o_ref[0] = _dot(s, w_ref[0], hi=True) + b_ref[0]


def _modulation(cond8, w_mod, b_mod):
    n_layers, d, n = w_mod.shape
    tn = n // 4
    return pl.pallas_call(
        _mod_kernel,
        grid=(n_layers, n // tn),
        in_specs=[pl.BlockSpec((8, d), lambda l, j: (0, 0)),
                  pl.BlockSpec((1, d, tn), lambda l, j: (l, 0, j)),
                  pl.BlockSpec((1, 1, tn), lambda l, j: (l, 0, j))],
        out_specs=pl.BlockSpec((1, 8, tn), lambda l, j: (l, 0, j)),
        out_shape=jax.ShapeDtypeStruct((n_layers, 8, n), F32),
        compiler_params=_cparams(),
        name="modulation",
    )(cond8, w_mod, b_mod.reshape(n_layers, 1, n))


def _mod_row(i, tm, seq_len, row0):
    if row0 == 0:
        return 0
    return row0 + (i * tm) // seq_len


def _norm_mm_kernel(x_ref, g_ref, mod_ref, w_ref, o_ref, h_ref, *, tm, seq_len, row0, sc_col, sh_col):
    i = pl.program_id(0)
    d = x_ref.shape[1]

    @pl.when(pl.program_id(1) == 0)
    def _():
        x = x_ref[...]
        y = x * lax.rsqrt(jnp.mean(x * x, axis=-1, keepdims=True) + NORM_EPS) * g_ref[...]
        row = _mod_row(i, tm, seq_len, row0)
        sc = mod_ref[pl.ds(row, 1), sc_col * d:(sc_col + 1) * d]
        sh = mod_ref[pl.ds(row, 1), sh_col * d:(sh_col + 1) * d]
        h_ref[...] = (y * (1.0 + sc) + sh).astype(BF16)

    o_ref[...] = jnp.dot(h_ref[...], w_ref[...], preferred_element_type=F32).astype(o_ref.dtype)


def _norm_mm(x, g, mod_l, w, *, tm, tn, seq_len, row0, sc_col, sh_col, name):
    t, d = x.shape
    n = w.shape[1]
    assert t % tm == 0 and n % tn == 0 and (row0 == 0 or seq_len % tm == 0)
    kern = functools.partial(_norm_mm_kernel, tm=tm, seq_len=seq_len, row0=row0, sc_col=sc_col, sh_col=sh_col)
    return pl.pallas_call(
        kern,
        grid=(t // tm, n // tn),
        in_specs=[pl.BlockSpec((tm, d), lambda i, j: (i, 0)),
                  pl.BlockSpec((1, d), lambda i, j: (0, 0)),
                  pl.BlockSpec(mod_l.shape, lambda i, j: (0, 0)),
                  pl.BlockSpec((d, tn), lambda i, j: (0, j))],
        out_specs=pl.BlockSpec((tm, tn), lambda i, j: (i, j)),
        out_shape=jax.ShapeDtypeStruct((t, n), BF16),
        scratch_shapes=[pltpu.VMEM((tm, d), BF16)],
        compiler_params=_cparams(),
        name=name,
    )(x, g.reshape(1, d), mod_l, w)


def _wkv_kernel(r_ref, k_ref, v_ref, zs_ref, wdec_ref, wicl_ref, w0_ref, a0_ref, kk_ref, ka_ref, rk_ref, s0_ref,
                y_ref, bonus_ref, sfin_ref, state_ref, *, reverse, n_chunks, n_pp):
    jt = pl.program_id(2)
    c = CHUNK

    @pl.when(jt == 0)
    def _():
        state_ref[...] = jnp.zeros(state_ref.shape, F32)
        for hi in range(2 * n_pp):
            lo = (hi % 2) * HEAD_DIM
            state_ref[hi // 2, lo:lo + HEAD_DIM, lo:lo + HEAD_DIM] = s0_ref[0, hi]

    zs = zs_ref[...].astype(F32)
    lane = lax.broadcasted_iota(jnp.int32, (PAIR, PAIR), 1)
    row = lax.broadcasted_iota(jnp.int32, (PAIR, PAIR), 0)
    own = (lane // HEAD_DIM) == (row // HEAD_DIM)
    head0_row = lax.broadcasted_iota(jnp.int32, (1, PAIR), 1) < HEAD_DIM

    def head_sum(x):
        s0 = jnp.sum(jnp.where(head0_row, x, 0.0), axis=-1, keepdims=True)
        s1 = jnp.sum(jnp.where(head0_row, 0.0, x), axis=-1, keepdims=True)
        return jnp.where(head0_row, s0, s1)

    def swap_heads(x):
        return pltpu.roll(x, HEAD_DIM, 1)

    if reverse:
        strict, incl = lane > row, lane >= row
    else:
        strict, incl = row > lane, row >= lane
    tri = incl[:c, :c].astype(F32)
    head0 = lax.broadcasted_iota(jnp.int32, (c, PAIR), 1) < HEAD_DIM

    w_lin = w0_ref[0] + _dot(jnp.tanh(zs), wdec_ref[0])
    softplus = jnp.maximum(-w_lin, 0.0) + jnp.log(1.0 + jnp.exp(-jnp.abs(w_lin)))
    log_decay = -jnp.exp(-softplus - 0.5)
    a_lr = _sigmoid(a0_ref[0] + _dot(zs, wicl_ref[0]))
    r = r_ref[...].astype(F32)
    k = k_ref[...].astype(F32)
    v = v_ref[...].astype(F32)
    kk = k * kk_ref[...]
    k_d = k * (1.0 + (a_lr - 1.0) * ka_ref[...])
    kk_sq = kk * kk
    rkr = r * k_d * rk_ref[...]
    lanes = lambda x, pi: x[:, pi * PAIR:(pi + 1) * PAIR]
    kk_norm = jnp.concatenate([jnp.sqrt(head_sum(lanes(kk_sq, pi))) for pi in range(n_pp)], axis=1)
    bonus = jnp.concatenate([head_sum(lanes(rkr, pi)) for pi in range(n_pp)], axis=1) * v
    bonus_ref[0] = bonus.astype(bonus_ref.dtype)
    kk = kk / jnp.maximum(kk_norm, L2_EPS)
    a_vec = -kk
    b_vec = kk * a_lr

    def stack(x):
        return jnp.concatenate([jnp.where(head0, x, 0.0), jnp.where(head0, 0.0, x)], axis=0)

    units = [(pi, ci) for ci in range(n_chunks) for pi in range(n_pp)]
    cut = lambda x, u: x[u[1] * c:(u[1] + 1) * c, u[0] * PAIR:(u[0] + 1) * PAIR]
    lw = [cut(log_decay, u) for u in units]
    cum = [_dot_split_rhs(tri, x) for x in lw]
    tot = [x[0:1] if reverse else x[c - 1:c] for x in cum]
    g_in = [jnp.exp(x) for x in cum]
    g_ex = [jnp.exp(x - y) for x, y in zip(cum, lw)]
    g_inv = [jnp.exp(-x) for x in cum]
    g_hat = [jnp.exp(t - x) for t, x in zip(tot, cum)]
    g_tot = [jnp.exp(t) for t in tot]
    aa = [stack(cut(a_vec, u) * g) for u, g in zip(units, g_ex)]
    ra = [stack(cut(r, u) * g) for u, g in zip(units, g_in)]
    bb = [stack(cut(b_vec, u) * g) for u, g in zip(units, g_inv)]
    kt = [stack(cut(k_d, u) * g) for u, g in zip(units, g_inv)]
    bh = [stack(cut(b_vec, u) * g) for u, g in zip(units, g_hat)]
    kh = [stack(cut(k_d, u) * g) for u, g in zip(units, g_hat)]
    vm = [stack(cut(v, u)) for u in units]
    scores = [_dot_nt(jnp.concatenate([a, b], axis=0), jnp.concatenate([e, f], axis=0))
              for a, b, e, f in zip(aa, ra, bb, kt)]
    n_pow = [jnp.where(strict, s[:PAIR, :PAIR], 0.0) for s in scores]
    n_ak = [jnp.where(strict, s[:PAIR, PAIR:], 0.0) for s in scores]
    m_rb = [jnp.where(incl, s[PAIR:, :PAIR], 0.0) for s in scores]
    m_rk = [jnp.where(incl, s[PAIR:, PAIR:], 0.0) for s in scores]
    xs = [a + swap_heads(_dot(n, w)) for a, n, w in zip(aa, n_ak, vm)]
    steps = CHUNK.bit_length() - 1
    for si in range(steps):
        if si + 1 < steps:
            prod = [_dot(n, jnp.concatenate([x, n], axis=1)) for n, x in zip(n_pow, xs)]
            xs = [x + p[:, :PAIR] for x, p in zip(xs, prod)]
            n_pow = [p[:, PAIR:] for p in prod]
        else:
            xs = [x + _dot(n, x) for n, x in zip(n_pow, xs)]
    vsw = [swap_heads(w) for w in vm]
    oc = [_dot(jnp.concatenate([mb, mk], axis=1), jnp.concatenate([x, w], axis=0))
          for mb, mk, x, w in zip(m_rb, m_rk, xs, vsw)]
    rh = [x + jnp.where(own, o, 0.0) for x, o in zip(ra, oc)]
    y_loc = [swap_heads(jnp.where(own, 0.0, o)) for o in oc]
    rq = [_dot_tn(jnp.concatenate([x, w], axis=0), jnp.concatenate([b, k_], axis=0))
          for x, w, b, k_ in zip(xs, vsw, bh, kh)]
    pp = [jnp.where(own, q, 0.0) for q in rq]
    qq = [jnp.where(own, jnp.concatenate([q[HEAD_DIM:], q[:HEAD_DIM]], axis=0), 0.0) for q in rq]

    state = [state_ref[pi] for pi in range(n_pp)]
    ys = {}
    for ci in (range(n_chunks - 1, -1, -1) if reverse else range(n_chunks)):
        for pi in range(n_pp):
            ui = units.index((pi, ci))
            ym = y_loc[ui] + _dot_nt(rh[ui], state[pi])
            state[pi] = state[pi] * g_tot[ui] + _dot(state[pi], pp[ui]) + qq[ui]
            ys[(pi, ci)] = ym[:c] + ym[c:]
    y_ref[0] = jnp.concatenate(
        [jnp.concatenate([ys[(pi, ci)] for pi in range(n_pp)], axis=1) for ci in range(n_chunks)],
        axis=0).astype(y_ref.dtype)
    for pi in range(n_pp):
        state_ref[pi] = state[pi]

    @pl.when(jt == pl.num_programs(2) - 1)
    def _():
        for hi in range(2 * n_pp):
            lo = (hi % 2) * HEAD_DIM
            sfin_ref[0, hi] = state[hi // 2][lo:lo + HEAD_DIM, lo:lo + HEAD_DIM]


def _wkv(z, s0p, wdec, wicl, w0, a0, k_k, k_a, r_k, *, n_seq, seq_len, tb, reverse, col_r, col_k, col_v, col_s):
    t = z.shape[0]
    n_pairs = k_k.shape[1] // PAIR
    nb = seq_len // tb
    d = 1 if reverse else 0
    n_pp = WKV_PAIRS_PER_STEP
    wl = n_pp * PAIR

    def tblk(b, j):
        return b * nb + ((nb - 1 - j) if reverse else j)

    kern = functools.partial(_wkv_kernel, reverse=reverse, n_chunks=tb // CHUNK, n_pp=n_pp)
    tok = lambda col: pl.BlockSpec((tb, wl), lambda b, p, j: (tblk(b, j), col // wl + p))
    vec = pl.BlockSpec((1, wl), lambda b, p, j: (0, p))
    dvec = pl.BlockSpec((1, 1, wl), lambda b, p, j: (d, 0, p))
    dmat = pl.BlockSpec((1, R_SMALL, wl), lambda b, p, j: (d, 0, p))
    st = pl.BlockSpec((1, 2 * n_pp, HEAD_DIM, HEAD_DIM), lambda b, p, j: (b, p, 0, 0))
    out_tok = pl.BlockSpec((1, tb, wl), lambda b, p, j: (0, tblk(b, j), p))
    y, bonus, sfin = pl.pallas_call(
        kern,
        grid=(n_seq, n_pairs // n_pp, nb),
        in_specs=[tok(col_r), tok(col_k), tok(col_v),
                  pl.BlockSpec((tb, R_SMALL), lambda b, p, j: (tblk(b, j), col_s // R_SMALL)),
                  dmat, dmat, dvec, dvec, vec, vec, vec, st],
        out_specs=[out_tok, out_tok, st],
        out_shape=[jax.ShapeDtypeStruct((1, t, n_pairs * PAIR), BF16),
                   jax.ShapeDtypeStruct((1, t, n_pairs * PAIR), BF16),
                   jax.ShapeDtypeStruct((n_seq, 2 * n_pairs, HEAD_DIM, HEAD_DIM), F32)],
        scratch_shapes=[pltpu.VMEM((n_pp, PAIR, PAIR), F32)],
        compiler_params=_cparams(),
        name="wkv_bwd" if reverse else "wkv_fwd",
    )(z, z, z, z, wdec, wicl, w0, a0, k_k, k_a, r_k, s0p)
    return y[0], bonus[0], sfin


def _shift_prev(x, period):
    t = lax.broadcasted_iota(jnp.int32, x.shape, 0)
    return jnp.where((t & (period - 1)) == 0, 0.0, pltpu.roll(x, 1, 0))


def _shift_next(x, period):
    t = lax.broadcasted_iota(jnp.int32, x.shape, 0)
    return jnp.where((t & (period - 1)) == period - 1, 0.0, pltpu.roll(x, x.shape[0] - 1, 0))


def _mix_kernel(x_ref, yf_ref, yb_ref, bf_ref, bb_ref, gd_ref, cb_ref, cc_ref, cx_ref, ga_ref, gb_ref, mod_ref,
                gnw_ref, gnb_ref, g2_ref, cw_ref, cbias_ref, wpa_ref, wpb_ref, wo_ref, o_ref,
                *, tm, seq_len, row0, period):
    i = pl.program_id(0)
    d = x_ref.shape[1]
    lane = lax.broadcasted_iota(jnp.int32, (PAIR, PAIR), 1)
    row = lax.broadcasted_iota(jnp.int32, (PAIR, PAIR), 0)
    head_mean = jnp.where((lane // HEAD_DIM) == (row // HEAD_DIM), 1.0 / HEAD_DIM, 0.0).astype(F32)

    y = yf_ref[...].astype(F32) + yb_ref[...].astype(F32)
    parts = []
    for s in range(d // PAIR):
        ys = y[:, s * PAIR:(s + 1) * PAIR]
        mu = _dot_split(ys, head_mean)
        yc = ys - mu
        var = _dot_split(yc * yc, head_mean)
        parts.append(yc * lax.rsqrt(var + GN_EPS))
    yn = jnp.concatenate(parts, axis=1)
    o = yn * gnw_ref[...] + gnb_ref[...] + bf_ref[...].astype(F32) + bb_ref[...].astype(F32)
    gate = _dot(_sigmoid(gd_ref[...].astype(F32)), g2_ref[...])
    o_a = o * gate

    p = cc_ref[...].astype(F32) * cx_ref[...].astype(F32)
    cw = cw_ref[...]
    conv = _shift_prev(p, period) * cw[0:1] + p * cw[1:2] + _shift_next(p, period) * cw[2:3] + cbias_ref[...]
    o_b = cb_ref[...].astype(F32) * conv

    merged = (_sigmoid(ga_ref[...].astype(F32)) * _dot(o_a, wpa_ref[...])
              + _sigmoid(gb_ref[...].astype(F32)) * _dot(o_b, wpb_ref[...]))
    mrow = _mod_row(i, tm, seq_len, row0)
    gt1 = mod_ref[pl.ds(mrow, 1), 2 * d:3 * d]
    o_ref[...] = x_ref[...] + gt1 * _dot(merged, wo_ref[...])


def _mix(x, yf, yb, bonf, bonb, z, mod_l, gn_w, gn_b, g2, conv_w, conv_b, w_pa, w_pb, w_o,
         *, tm, seq_len, row0, period, cols):
    t, d = x.shape
    kern = functools.partial(_mix_kernel, tm=tm, seq_len=seq_len, row0=row0, period=period)
    tok = pl.BlockSpec((tm, d), lambda i: (i, 0))
    zcol = lambda col: pl.BlockSpec((tm, d), lambda i: (i, col // d))
    full = lambda a: pl.BlockSpec(a.shape, lambda i: (0,) * a.ndim)
    vec = pl.BlockSpec((1, d), lambda i: (0, 0))
    return pl.pallas_call(
        kern,
        grid=(t // tm,),
        in_specs=[tok, tok, tok, tok, tok,
                  pl.BlockSpec((tm, R_G), lambda i: (i, cols["gd"] // R_G)),
                  zcol(cols["cb"]), zcol(cols["cc"]), zcol(cols["cx"]), zcol(cols["ga"]), zcol(cols["gb"]),
                  full(mod_l), vec, vec, full(g2), full(conv_w), vec, full(w_pa), full(w_pb), full(w_o)],
        out_specs=tok,
        out_shape=jax.ShapeDtypeStruct((t, d), F32),
        compiler_params=_cparams(),
        name="mix",
    )(x, yf, yb, bonf, bonb, z, z, z, z, z, z, mod_l, gn_w.reshape(1, d), gn_b.reshape(1, d), g2, conv_w,
      conv_b.reshape(1, d), w_pa, w_pb, w_o)


def _ffn_kernel(x_ref, u_ref, up_ref, un_ref, mod_ref, cw_ref, cb_ref, wd_ref, gf_ref, o_ref,
                *, tm, seq_len, row0, grid, final, kc):
    i = pl.program_id(0)
    d = x_ref.shape[1]
    d_ff = wd_ref.shape[0]
    tiles_per_seq = seq_len // tm
    first = (i % tiles_per_seq) == 0
    last = (i % tiles_per_seq) == tiles_per_seq - 1

    def conv(col):
        u = u_ref[:, col:col + kc].astype(F32)
        if grid:
            halo_p = jnp.where(first, 0.0, up_ref[:, col:col + kc].astype(F32))
            halo_n = jnp.where(last, 0.0, un_ref[:, col:col + kc].astype(F32))
            prev = jnp.concatenate([halo_p, u[:tm - GRID_W]], axis=0)
            nxt = jnp.concatenate([u[GRID_W:], halo_n], axis=0)
        else:
            prev = _shift_prev(u, seq_len)
            nxt = _shift_next(u, seq_len)
        cw = cw_ref[:, col:col + kc]
        return prev * cw[0:1] + u * cw[1:2] + nxt * cw[2:3] + cb_ref[:, col:col + kc]

    acc = jnp.zeros((tm, d), F32)
    for ci in range(d_ff // kc):
        act = conv(ci * kc)
        lin = conv(d_ff + ci * kc)
        acc = acc + _dot(_silu(act) * lin, wd_ref[ci * kc:(ci + 1) * kc, :])
    mrow = _mod_row(i, tm, seq_len, row0)
    gt2 = mod_ref[pl.ds(mrow, 1), 5 * d:6 * d]
    xn = x_ref[...] + gt2 * acc
    if final:
        xn = xn * lax.rsqrt(jnp.mean(xn * xn, axis=-1, keepdims=True) + NORM_EPS) * gf_ref[...]
    o_ref[...] = xn


def _ffn(x, u, mod_l, conv_w, conv_b, w_down, norm_f_g, *, tm, seq_len, row0, grid, final):
    t, d = x.shape
    n_up = u.shape[1]
    hb = tm // GRID_W
    n_halo = t // GRID_W
    kern = functools.partial(_ffn_kernel, tm=tm, seq_len=seq_len, row0=row0, grid=grid, final=final,
                             kc=FFN_COL_CHUNK)
    full = lambda a: pl.BlockSpec(a.shape, lambda i: (0,) * a.ndim)
    return pl.pallas_call(
        kern,
        grid=(t // tm,),
        in_specs=[pl.BlockSpec((tm, d), lambda i: (i, 0)),
                  pl.BlockSpec((tm, n_up), lambda i: (i, 0)),
                  pl.BlockSpec((GRID_W, n_up), lambda i: (jnp.maximum(i * hb - 1, 0), 0)),
                  pl.BlockSpec((GRID_W, n_up), lambda i: (jnp.minimum(i * hb + hb, n_halo - 1), 0)),
                  full(mod_l), full(conv_w), pl.BlockSpec((1, n_up), lambda i: (0, 0)), full(w_down),
                  pl.BlockSpec((1, d), lambda i: (0, 0))],
        out_specs=pl.BlockSpec((tm, d), lambda i: (i, 0)),
        out_shape=jax.ShapeDtypeStruct((t, d), F32),
        compiler_params=_cparams(),
        name="ffn",
    )(x, u, u, u, mod_l, conv_w, conv_b.reshape(1, n_up), w_down, norm_f_g.reshape(1, d))


def kernel(x_prompt, x_sample, state_wkv, c, c_ctx, w_mod, b_mod, norm1_g, w_in, decay_w0, decay_w2, iclr_a0,
           iclr_a2, gate_g2, k_k, k_a, r_k, gn_w, gn_b, conv_mix_w, conv_mix_b, w_pa, w_pb, w_o, norm2_g, w_up,
           conv_ffn_w, conv_ffn_b, w_down, norm_f_g):
    n_ctx, ctx_len, d = x_prompt.shape
    n_lat, lat_len, _ = x_sample.shape
    n_layers = w_mod.shape[0]
    d_a = k_k.shape[1]
    n_heads = d_a // HEAD_DIM
    r_w = decay_w2.shape[2]
    assert d_a == d and 4 * r_w == R_SMALL and gate_g2.shape[1] == R_G

    o_small = 3 * d_a
    o_gd = o_small + R_SMALL
    o_cb = o_gd + R_G
    o_gates = o_cb + 3 * d
    cols = dict(r=0, k=d, v=2 * d, cb=3 * d, cc=4 * d, cx=5 * d, ga=6 * d, gb=7 * d, small=8 * d,
                gd=8 * d + R_SMALL)
    n_in = 8 * d + R_SMALL + R_G
    n_in_pad = 8 * d + 512
    w_in_p = jnp.concatenate(
        [w_in[:, :, :o_small], w_in[:, :, o_cb:o_gates], w_in[:, :, o_gates:], w_in[:, :, o_small:o_cb],
         jnp.zeros((n_layers, d, n_in_pad - n_in), w_in.dtype)], axis=2).astype(BF16)
    w_up_b = w_up.astype(BF16)
    w_pa_b, w_pb_b, w_o_b = w_pa.astype(BF16), w_pb.astype(BF16), w_o.astype(BF16)
    w_down_b = w_down.astype(BF16)
    g2_b = gate_g2.astype(BF16)

    zrow = jnp.zeros((n_layers, r_w, d_a), F32)
    wdec = jnp.stack([jnp.concatenate([decay_w2[:, 0], zrow, zrow, zrow], axis=1),
                      jnp.concatenate([zrow, decay_w2[:, 1], zrow, zrow], axis=1)], axis=1)
    wicl = jnp.stack([jnp.concatenate([zrow, zrow, iclr_a2[:, 0], zrow], axis=1),
                      jnp.concatenate([zrow, zrow, zrow, iclr_a2[:, 1]], axis=1)], axis=1)

    cond8 = jnp.concatenate([c_ctx[None, :], c, jnp.zeros((8 - 1 - n_lat, d), F32)], axis=0)
    mod = _modulation(cond8, w_mod, b_mod)

    def layer(x, l, *, n_seq, seq_len, row0, grid, s0, final):
        tm = TOKEN_TILE
        z = _norm_mm(x, norm1_g[l], mod[l], w_in_p[l], tm=MM_TOKEN_TILE, tn=n_in_pad // 4, seq_len=seq_len,
                     row0=row0, sc_col=1, sh_col=0, name="in_proj")
        outs = []
        for rev in (False, True):
            dd = 1 if rev else 0
            outs.append(_wkv(z, s0[dd], wdec[l], wicl[l], decay_w0[l].reshape(2, 1, d_a),
                             iclr_a0[l].reshape(2, 1, d_a), k_k[l].reshape(1, d_a), k_a[l].reshape(1, d_a),
                             r_k[l].reshape(1, d_a), n_seq=n_seq, seq_len=seq_len, tb=WKV_BLOCK, reverse=rev,
                             col_r=cols["r"], col_k=cols["k"], col_v=cols["v"], col_s=cols["small"]))
        (yf, bonf, sf), (yb, bonb, sb) = outs
        x = _mix(x, yf, yb, bonf, bonb, z, mod[l], gn_w[l], gn_b[l], g2_b[l], conv_mix_w[l], conv_mix_b[l],
                 w_pa_b[l], w_pb_b[l], w_o_b[l], tm=tm, seq_len=seq_len, row0=row0,
                 period=GRID_W if grid else seq_len, cols=cols)
        u = _norm_mm(x, norm2_g[l], mod[l], w_up_b[l], tm=MM_TOKEN_TILE, tn=w_up.shape[2] // 4, seq_len=seq_len,
                     row0=row0, sc_col=4, sh_col=3, name="ffn_up")
        x = _ffn(x, u, mod[l], conv_ffn_w[l], conv_ffn_b[l], w_down_b[l], norm_f_g, tm=tm, seq_len=seq_len,
                 row0=row0, grid=grid, final=final)
        return x, sf, sb

    xp = x_prompt.reshape(n_ctx * ctx_len, d)
    zero_state = jnp.zeros((n_ctx, n_heads, HEAD_DIM, HEAD_DIM), F32)
    ctx_states = []
    for l in range(n_layers):
        xp, sf, sb = layer(xp, l, n_seq=n_ctx, seq_len=ctx_len, row0=0, grid=False,
                           s0=(zero_state, zero_state), final=(l == n_layers - 1))
        ctx_states.append(jnp.stack([sf, sb], axis=1))
    new_state = jnp.stack(ctx_states, axis=1).astype(x_prompt.dtype)
    y_prompt = xp.reshape(n_ctx, ctx_len, d)

    xs = x_sample.reshape(n_lat * lat_len, d)
    for l in range(n_layers):
        s0 = (state_wkv[:, l, 0].astype(F32), state_wkv[:, l, 1].astype(F32))
        xs, _, _ = layer(xs, l, n_seq=n_lat, seq_len=lat_len, row0=1, grid=True, s0=s0,
                         final=(l == n_layers - 1))
    y_sample = xs.reshape(n_lat, lat_len, d)
    return (y_prompt, y_sample, new_state)
```

```python
import functools

import jax
import jax.numpy as jnp
from jax import lax
from jax.experimental import pallas as pl
from jax.experimental.pallas import tpu as pltpu

F32 = jnp.float32
BF16 = jnp.bfloat16
HIGHEST = lax.Precision.HIGHEST

HEAD_DIM = 64
PAIR = 2 * HEAD_DIM
CHUNK = 64
WKV_PAIRS_PER_STEP = 4
WKV_BLOCK = 256
TOKEN_TILE = 256
MM_TOKEN_TILE = 1024
FFN_COL_CHUNK = 256
GRID_W = 64
R_SMALL = 256
R_G = 128
NORM_EPS = 1e-6
GN_EPS = 64e-5
L2_EPS = 1e-12
VMEM_LIMIT = 56 * 1024 * 1024


def _cparams():
    return pltpu.CompilerParams(vmem_limit_bytes=VMEM_LIMIT)


def _dot(a, b, hi=False):
    if hi:
        return jnp.dot(a, b, precision=HIGHEST, preferred_element_type=F32)
    return jnp.dot(a.astype(BF16), b.astype(BF16), preferred_element_type=F32)


def _dot_nt(a, b):
    return lax.dot_general(a.astype(BF16), b.astype(BF16), (((1,), (1,)), ((), ())),
                           preferred_element_type=F32)


def _dot_tn(a, b):
    return jnp.dot(a.T.astype(BF16), b.astype(BF16), preferred_element_type=F32)


def _split(x):
    hi = x.astype(BF16)
    return hi, (x - hi.astype(F32)).astype(BF16)


def _dot_split(a, b):
    hi, lo = _split(a)
    bb = b.astype(BF16)
    return jnp.dot(hi, bb, preferred_element_type=F32) + jnp.dot(lo, bb, preferred_element_type=F32)


def _dot_split_rhs(a, b):
    hi, lo = _split(b)
    ab = a.astype(BF16)
    return jnp.dot(ab, hi, preferred_element_type=F32) + jnp.dot(ab, lo, preferred_element_type=F32)


def _sigmoid(x):
    return 1.0 / (1.0 + jnp.exp(-x))


def _silu(x):
    return x * _sigmoid(x)


def _mod_kernel(c_ref, w_ref, b_ref, o_ref):
    s = _silu(c_ref[...])
    o_ref[0] = _dot(s, w_ref[0], hi=True) + b_ref[0]


def _modulation(cond8, w_mod, b_mod):
    n_layers, d, n = w_mod.shape
    tn = n // 4
    return pl.pallas_call(
        _mod_kernel,
        grid=(n_layers, n // tn),
        in_specs=[pl.BlockSpec((8, d), lambda l, j: (0, 0)),
                  pl.BlockSpec((1, d, tn), lambda l, j: (l, 0, j)),
                  pl.BlockSpec((1, 1, tn), lambda l, j: (l, 0, j))],
        out_specs=pl.BlockSpec((1, 8, tn), lambda l, j: (l, 0, j)),
        out_shape=jax.ShapeDtypeStruct((n_layers, 8, n), F32),
        compiler_params=_cparams(),
        name="modulation",
    )(cond8, w_mod, b_mod.reshape(n_layers, 1, n))


def _mod_row(i, tm, seq_len, row0):
    if row0 == 0:
        return 0
    return row0 + (i * tm) // seq_len


def _norm_mm_kernel(x_ref, g_ref, mod_ref, w_ref, o_ref, h_ref, *, tm, seq_len, row0, sc_col, sh_col):
    i = pl.program_id(0)
    d = x_ref.shape[1]

    @pl.when(pl.program_id(1) == 0)
    def _():
        x = x_ref[...]
        y = x * lax.rsqrt(jnp.mean(x * x, axis=-1, keepdims=True) + NORM_EPS) * g_ref[...]
        row = _mod_row(i, tm, seq_len, row0)
        sc = mod_ref[pl.ds(row, 1), sc_col * d:(sc_col + 1) * d]
        sh = mod_ref[pl.ds(row, 1), sh_col * d:(sh_col + 1) * d]
        h_ref[...] = (y * (1.0 + sc) + sh).astype(BF16)

    o_ref[...] = jnp.dot(h_ref[...], w_ref[...], preferred_element_type=F32).astype(o_ref.dtype)


def _norm_mm(x, g, mod_l, w, *, tm, tn, seq_len, row0, sc_col, sh_col, name):
    t, d = x.shape
    n = w.shape[1]
    assert t % tm == 0 and n % tn == 0 and (row0 == 0 or seq_len % tm == 0)
    kern = functools.partial(_norm_mm_kernel, tm=tm, seq_len=seq_len, row0=row0, sc_col=sc_col, sh_col=sh_col)
    return pl.pallas_call(
        kern,
        grid=(t // tm, n // tn),
        in_specs=[pl.BlockSpec((tm, d), lambda i, j: (i, 0)),
                  pl.BlockSpec((1, d), lambda i, j: (0, 0)),
                  pl.BlockSpec(mod_l.shape, lambda i, j: (0, 0)),
                  pl.BlockSpec((d, tn), lambda i, j: (0, j))],
        out_specs=pl.BlockSpec((tm, tn), lambda i, j: (i, j)),
        out_shape=jax.ShapeDtypeStruct((t, n), BF16),
        scratch_shapes=[pltpu.VMEM((tm, d), BF16)],
        compiler_params=_cparams(),
        name=name,
    )(x, g.reshape(1, d), mod_l, w)


def _wkv_kernel(r_ref, k_ref, v_ref, zs_ref, wdec_ref, wicl_ref, w0_ref, a0_ref, kk_ref, ka_ref, rk_ref, s0_ref,
                y_ref, bonus_ref, sfin_ref, state_ref, prep_ref, loc_ref, gtot_ref, *, reverse, n_chunks, n_pp):
    jt = pl.program_id(2)
    c = CHUNK
    half = n_chunks // 2
    order = list(range(n_chunks - 1, -1, -1) if reverse else range(n_chunks))
    units_a = [(pi, ci) for ci in order[:half] for pi in range(n_pp)]
    units_b = [(pi, ci) for ci in order[half:] for pi in range(n_pp)]
    row_b = min(order[half:]) * c

    @pl.when(jt == 0)
    def _():
        state_ref[...] = jnp.zeros(state_ref.shape, F32)
        for hi in range(2 * n_pp):
            lo = (hi % 2) * HEAD_DIM
            state_ref[hi // 2, lo:lo + HEAD_DIM, lo:lo + HEAD_DIM] = s0_ref[0, hi]

    lane = lax.broadcasted_iota(jnp.int32, (PAIR, PAIR), 1)
    row = lax.broadcasted_iota(jnp.int32, (PAIR, PAIR), 0)
    own = (lane // HEAD_DIM) == (row // HEAD_DIM)
    head0_row = lax.broadcasted_iota(jnp.int32, (1, PAIR), 1) < HEAD_DIM
    if reverse:
        strict, incl = lane > row, lane >= row
    else:
        strict, incl = row > lane, row >= lane
    head0 = lax.broadcasted_iota(jnp.int32, (c, PAIR), 1) < HEAD_DIM

    def head_sum(x):
        s0 = jnp.sum(jnp.where(head0_row, x, 0.0), axis=-1, keepdims=True)
        s1 = jnp.sum(jnp.where(head0_row, 0.0, x), axis=-1, keepdims=True)
        return jnp.where(head0_row, s0, s1)

    def swap_heads(x):
        return pltpu.roll(x, HEAD_DIM, 1)

    def stack(x):
        return jnp.concatenate([jnp.where(head0, x, 0.0), jnp.where(head0, 0.0, x)], axis=0)

    def chunk_local(units, src, row0):
        r, k_d, v, a_vec, b_vec, log_decay = src
        tri = incl[:c, :c].astype(F32)
        cut = lambda x, u: x[u[1] * c - row0:(u[1] + 1) * c - row0, u[0] * PAIR:(u[0] + 1) * PAIR]
        lw = [cut(log_decay, u) for u in units]
        cum = [_dot_split_rhs(tri, x) for x in lw]
        tot = [x[0:1] if reverse else x[c - 1:c] for x in cum]
        g_in = [jnp.exp(x) for x in cum]
        g_ex = [jnp.exp(x - y) for x, y in zip(cum, lw)]
        g_inv = [jnp.exp(-x) for x in cum]
        g_hat = [jnp.exp(t - x) for t, x in zip(tot, cum)]
        g_tot = [jnp.exp(t) for t in tot]
        aa = [stack(cut(a_vec, u) * g) for u, g in zip(units, g_ex)]
        ra = [stack(cut(r, u) * g) for u, g in zip(units, g_in)]
        bb = [stack(cut(b_vec, u) * g) for u, g in zip(units, g_inv)]
        kt = [stack(cut(k_d, u) * g) for u, g in zip(units, g_inv)]
        bh = [stack(cut(b_vec, u) * g) for u, g in zip(units, g_hat)]
        kh = [stack(cut(k_d, u) * g) for u, g in zip(units, g_hat)]
        vm = [stack(cut(v, u)) for u in units]
        scores = [_dot_nt(jnp.concatenate([a, b], axis=0), jnp.concatenate([e, f], axis=0))
                  for a, b, e, f in zip(aa, ra, bb, kt)]
        n_pow = [jnp.where(strict, s[:PAIR, :PAIR], 0.0) for s in scores]
        n_ak = [jnp.where(strict, s[:PAIR, PAIR:], 0.0) for s in scores]
        m_rb = [jnp.where(incl, s[PAIR:, :PAIR], 0.0) for s in scores]
        m_rk = [jnp.where(incl, s[PAIR:, PAIR:], 0.0) for s in scores]
        xs = [a + swap_heads(_dot(n, w)) for a, n, w in zip(aa, n_ak, vm)]
        steps = CHUNK.bit_length() - 1
        for si in range(steps):
            if si + 1 < steps:
                prod = [_dot(n, jnp.concatenate([x, n], axis=1)) for n, x in zip(n_pow, xs)]
                xs = [x + p[:, :PAIR] for x, p in zip(xs, prod)]
                n_pow = [p[:, PAIR:] for p in prod]
            else:
                xs = [x + _dot(n, x) for n, x in zip(n_pow, xs)]
        vsw = [swap_heads(w) for w in vm]
        oc = [_dot(jnp.concatenate([mb, mk], axis=1), jnp.concatenate([x, w], axis=0))
              for mb, mk, x, w in zip(m_rb, m_rk, xs, vsw)]
        rh = [x + jnp.where(own, o, 0.0) for x, o in zip(ra, oc)]
        y_loc = [swap_heads(jnp.where(own, 0.0, o)) for o in oc]
        rq = [_dot_tn(jnp.concatenate([x, w], axis=0), jnp.concatenate([b, k_], axis=0))
              for x, w, b, k_ in zip(xs, vsw, bh, kh)]
        pp = [jnp.where(own, q, 0.0) for q in rq]
        qq = [jnp.where(own, jnp.concatenate([q[HEAD_DIM:], q[:HEAD_DIM]], axis=0), 0.0) for q in rq]
        return g_tot, pp, qq, rh, y_loc

    @pl.when(jt >= 0)
    def _wave_a():
        zs = zs_ref[...].astype(F32)
        w_lin = w0_ref[0] + _dot(jnp.tanh(zs), wdec_ref[0])
        softplus = jnp.maximum(-w_lin, 0.0) + jnp.log(1.0 + jnp.exp(-jnp.abs(w_lin)))
        log_decay = -jnp.exp(-softplus - 0.5)
        a_lr = _sigmoid(a0_ref[0] + _dot(zs, wicl_ref[0]))
        r = r_ref[...].astype(F32)
        k = k_ref[...].astype(F32)
        v = v_ref[...].astype(F32)
        kk = k * kk_ref[...]
        k_d = k * (1.0 + (a_lr - 1.0) * ka_ref[...])
        kk_sq = kk * kk
        rkr = r * k_d * rk_ref[...]
        lanes = lambda x, pi: x[:, pi * PAIR:(pi + 1) * PAIR]
        kk_norm = jnp.concatenate([jnp.sqrt(head_sum(lanes(kk_sq, pi))) for pi in range(n_pp)], axis=1)
        bonus = jnp.concatenate([head_sum(lanes(rkr, pi)) for pi in range(n_pp)], axis=1) * v
        bonus_ref[0] = bonus.astype(bonus_ref.dtype)
        kk = kk / jnp.maximum(kk_norm, L2_EPS)
        src = (r, k_d, v, -kk, kk * a_lr, log_decay)
        for i, x in enumerate(src):
            prep_ref[i] = x[row_b:row_b + half * c]
        for ui, vals in enumerate(zip(*chunk_local(units_a, src, 0))):
            gtot_ref[ui] = vals[0]
            for i in range(4):
                loc_ref[ui, i] = vals[i + 1]

    @pl.when(jt < pl.num_programs(2))
    def _wave_b():
        src = tuple(prep_ref[i] for i in range(6))
        local_b = list(zip(*chunk_local(units_b, src, row_b)))
        state = [state_ref[pi] for pi in range(n_pp)]
        ys = {}
        for ci in order:
            for pi in range(n_pp):
                if (pi, ci) in units_a:
                    ui = units_a.index((pi, ci))
                    g_tot, pp, qq, rh, y_loc = (gtot_ref[ui], loc_ref[ui, 0], loc_ref[ui, 1], loc_ref[ui, 2],
                                                loc_ref[ui, 3])
                else:
                    g_tot, pp, qq, rh, y_loc = local_b[units_b.index((pi, ci))]
                ym = y_loc + _dot_nt(rh, state[pi])
                state[pi] = state[pi] * g_tot + _dot(state[pi], pp) + qq
                ys[(pi, ci)] = ym[:c] + ym[c:]
        y_ref[0] = jnp.concatenate(
            [jnp.concatenate([ys[(pi, ci)] for pi in range(n_pp)], axis=1) for ci in range(n_chunks)],
            axis=0).astype(y_ref.dtype)
        for pi in range(n_pp):
            state_ref[pi] = state[pi]

        @pl.when(jt == pl.num_programs(2) - 1)
        def _():
            for hi in range(2 * n_pp):
                lo = (hi % 2) * HEAD_DIM
                sfin_ref[0, hi] = state[hi // 2][lo:lo + HEAD_DIM, lo:lo + HEAD_DIM]


def _wkv(z, s0p, wdec, wicl, w0, a0, k_k, k_a, r_k, *, n_seq, seq_len, tb, reverse, col_r, col_k, col_v, col_s):
    t = z.shape[0]
    n_pairs = k_k.shape[1] // PAIR
    nb = seq_len // tb
    d = 1 if reverse else 0
    n_pp = WKV_PAIRS_PER_STEP
    wl = n_pp * PAIR
    n_a = n_pp * (tb // CHUNK // 2)
    assert (tb // CHUNK) % 2 == 0 and n_pairs % n_pp == 0 and seq_len % tb == 0

    def tblk(b, j):
        return b * nb + ((nb - 1 - j) if reverse else j)

    kern = functools.partial(_wkv_kernel, reverse=reverse, n_chunks=tb // CHUNK, n_pp=n_pp)
    tok = lambda col: pl.BlockSpec((tb, wl), lambda b, p, j: (tblk(b, j), col // wl + p))
    vec = pl.BlockSpec((1, wl), lambda b, p, j: (0, p))
    dvec = pl.BlockSpec((1, 1, wl), lambda b, p, j: (d, 0, p))
    dmat = pl.BlockSpec((1, R_SMALL, wl), lambda b, p, j: (d, 0, p))
    st = pl.BlockSpec((1, 2 * n_pp, HEAD_DIM, HEAD_DIM), lambda b, p, j: (b, p, 0, 0))
    out_tok = pl.BlockSpec((1, tb, wl), lambda b, p, j: (0, tblk(b, j), p))
    y, bonus, sfin = pl.pallas_call(
        kern,
        grid=(n_seq, n_pairs // n_pp, nb),
        in_specs=[tok(col_r), tok(col_k), tok(col_v),
                  pl.BlockSpec((tb, R_SMALL), lambda b, p, j: (tblk(b, j), col_s // R_SMALL)),
                  dmat, dmat, dvec, dvec, vec, vec, vec, st],
        out_specs=[out_tok, out_tok, st],
        out_shape=[jax.ShapeDtypeStruct((1, t, n_pairs * PAIR), BF16),
                   jax.ShapeDtypeStruct((1, t, n_pairs * PAIR), BF16),
                   jax.ShapeDtypeStruct((n_seq, 2 * n_pairs, HEAD_DIM, HEAD_DIM), F32)],
        scratch_shapes=[pltpu.VMEM((n_pp, PAIR, PAIR), F32),
                        pltpu.VMEM((6, tb // 2, wl), F32),
                        pltpu.VMEM((n_a, 4, PAIR, PAIR), F32),
                        pltpu.VMEM((n_a, 1, PAIR), F32)],
        compiler_params=_cparams(),
        name="wkv_bwd" if reverse else "wkv_fwd",
    )(z, z, z, z, wdec, wicl, w0, a0, k_k, k_a, r_k, s0p)
    return y[0], bonus[0], sfin


def _shift_prev(x, period):
    t = lax.broadcasted_iota(jnp.int32, x.shape, 0)
    return jnp.where((t & (period - 1)) == 0, 0.0, pltpu.roll(x, 1, 0))


def _shift_next(x, period):
    t = lax.broadcasted_iota(jnp.int32, x.shape, 0)
    return jnp.where((t & (period - 1)) == period - 1, 0.0, pltpu.roll(x, x.shape[0] - 1, 0))


def _mix_kernel(x_ref, yf_ref, yb_ref, bf_ref, bb_ref, gd_ref, cb_ref, cc_ref, cx_ref, ga_ref, gb_ref, mod_ref,
                gnw_ref, gnb_ref, g2_ref, cw_ref, cbias_ref, wpa_ref, wpb_ref, wo_ref, o_ref,
                *, tm, seq_len, row0, period):
    i = pl.program_id(0)
    d = x_ref.shape[1]
    lane = lax.broadcasted_iota(jnp.int32, (PAIR, PAIR), 1)
    row = lax.broadcasted_iota(jnp.int32, (PAIR, PAIR), 0)
    head_mean = jnp.where((lane // HEAD_DIM) == (row // HEAD_DIM), 1.0 / HEAD_DIM, 0.0).astype(F32)

    y = yf_ref[...].astype(F32) + yb_ref[...].astype(F32)
    parts = []
    for s in range(d // PAIR):
        ys = y[:, s * PAIR:(s + 1) * PAIR]
        mu = _dot_split(ys, head_mean)
        yc = ys - mu
        var = _dot_split(yc * yc, head_mean)
        parts.append(yc * lax.rsqrt(var + GN_EPS))
    yn = jnp.concatenate(parts, axis=1)
    o = yn * gnw_ref[...] + gnb_ref[...] + bf_ref[...].astype(F32) + bb_ref[...].astype(F32)
    gate = _dot(_sigmoid(gd_ref[...].astype(F32)), g2_ref[...])
    o_a = o * gate

    p = cc_ref[...].astype(F32) * cx_ref[...].astype(F32)
    cw = cw_ref[...]
    conv = _shift_prev(p, period) * cw[0:1] + p * cw[1:2] + _shift_next(p, period) * cw[2:3] + cbias_ref[...]
    o_b = cb_ref[...].astype(F32) * conv

    merged = (_sigmoid(ga_ref[...].astype(F32)) * _dot(o_a, wpa_ref[...])
              + _sigmoid(gb_ref[...].astype(F32)) * _dot(o_b, wpb_ref[...]))
    mrow = _mod_row(i, tm, seq_len, row0)
    gt1 = mod_ref[pl.ds(mrow, 1), 2 * d:3 * d]
    o_ref[...] = x_ref[...] + gt1 * _dot(merged, wo_ref[...])


def _mix(x, yf, yb, bonf, bonb, z, mod_l, gn_w, gn_b, g2, conv_w, conv_b, w_pa, w_pb, w_o,
         *, tm, seq_len, row0, period, cols):
    t, d = x.shape
    kern = functools.partial(_mix_kernel, tm=tm, seq_len=seq_len, row0=row0, period=period)
    tok = pl.BlockSpec((tm, d), lambda i: (i, 0))
    zcol = lambda col: pl.BlockSpec((tm, d), lambda i: (i, col // d))
    full = lambda a: pl.BlockSpec(a.shape, lambda i: (0,) * a.ndim)
    vec = pl.BlockSpec((1, d), lambda i: (0, 0))
    return pl.pallas_call(
        kern,
        grid=(t // tm,),
        in_specs=[tok, tok, tok, tok, tok,
                  pl.BlockSpec((tm, R_G), lambda i: (i, cols["gd"] // R_G)),
                  zcol(cols["cb"]), zcol(cols["cc"]), zcol(cols["cx"]), zcol(cols["ga"]), zcol(cols["gb"]),
                  full(mod_l), vec, vec, full(g2), full(conv_w), vec, full(w_pa), full(w_pb), full(w_o)],
        out_specs=tok,
        out_shape=jax.ShapeDtypeStruct((t, d), F32),
        compiler_params=_cparams(),
        name="mix",
    )(x, yf, yb, bonf, bonb, z, z, z, z, z, z, mod_l, gn_w.reshape(1, d), gn_b.reshape(1, d), g2, conv_w,
      conv_b.reshape(1, d), w_pa, w_pb, w_o)


def _ffn_kernel(x_ref, u_ref, up_ref, un_ref, mod_ref, cw_ref, cb_ref, wd_ref, gf_ref, o_ref,
                *, tm, seq_len, row0, grid, final, kc):
    i = pl.program_id(0)
    d = x_ref.shape[1]
    d_ff = wd_ref.shape[0]
    tiles_per_seq = seq_len // tm
    first = (i % tiles_per_seq) == 0
    last = (i % tiles_per_seq) == tiles_per_seq - 1

    def conv(col):
        u = u_ref[:, col:col + kc].astype(F32)
        if grid:
            halo_p = jnp.where(first, 0.0, up_ref[:, col:col + kc].astype(F32))
            halo_n = jnp.where(last, 0.0, un_ref[:, col:col + kc].astype(F32))
            prev = jnp.concatenate([halo_p, u[:tm - GRID_W]], axis=0)
            nxt = jnp.concatenate([u[GRID_W:], halo_n], axis=0)
        else:
            prev = _shift_prev(u, seq_len)
            nxt = _shift_next(u, seq_len)
        cw = cw_ref[:, col:col + kc]
        return prev * cw[0:1] + u * cw[1:2] + nxt * cw[2:3] + cb_ref[:, col:col + kc]

    acc = jnp.zeros((tm, d), F32)
    for ci in range(d_ff // kc):
        act = conv(ci * kc)
        lin = conv(d_ff + ci * kc)
        acc = acc + _dot(_silu(act) * lin, wd_ref[ci * kc:(ci + 1) * kc, :])
    mrow = _mod_row(i, tm, seq_len, row0)
    gt2 = mod_ref[pl.ds(mrow, 1), 5 * d:6 * d]
    xn = x_ref[...] + gt2 * acc
    if final:
        xn = xn * lax.rsqrt(jnp.mean(xn * xn, axis=-1, keepdims=True) + NORM_EPS) * gf_ref[...]
    o_ref[...] = xn


def _ffn(x, u, mod_l, conv_w, conv_b, w_down, norm_f_g, *, tm, seq_len, row0, grid, final):
    t, d = x.shape
    n_up = u.shape[1]
    hb = tm // GRID_W
    n_halo = t // GRID_W
    kern = functools.partial(_ffn_kernel, tm=tm, seq_len=seq_len, row0=row0, grid=grid, final=final,
                             kc=FFN_COL_CHUNK)
    full = lambda a: pl.BlockSpec(a.shape, lambda i: (0,) * a.ndim)
    return pl.pallas_call(
        kern,
        grid=(t // tm,),
        in_specs=[pl.BlockSpec((tm, d), lambda i: (i, 0)),
                  pl.BlockSpec((tm, n_up), lambda i: (i, 0)),
                  pl.BlockSpec((GRID_W, n_up), lambda i: (jnp.maximum(i * hb - 1, 0), 0)),
                  pl.BlockSpec((GRID_W, n_up), lambda i: (jnp.minimum(i * hb + hb, n_halo - 1), 0)),
                  full(mod_l), full(conv_w), pl.BlockSpec((1, n_up), lambda i: (0, 0)), full(w_down),
                  pl.BlockSpec((1, d), lambda i: (0, 0))],
        out_specs=pl.BlockSpec((tm, d), lambda i: (i, 0)),
        out_shape=jax.ShapeDtypeStruct((t, d), F32),
        compiler_params=_cparams(),
        name="ffn",
    )(x, u, u, u, mod_l, conv_w, conv_b.reshape(1, n_up), w_down, norm_f_g.reshape(1, d))


def kernel(x_prompt, x_sample, state_wkv, c, c_ctx, w_mod, b_mod, norm1_g, w_in, decay_w0, decay_w2, iclr_a0,
           iclr_a2, gate_g2, k_k, k_a, r_k, gn_w, gn_b, conv_mix_w, conv_mix_b, w_pa, w_pb, w_o, norm2_g, w_up,
           conv_ffn_w, conv_ffn_b, w_down, norm_f_g):
    n_ctx, ctx_len, d = x_prompt.shape
    n_lat, lat_len, _ = x_sample.shape
    n_layers = w_mod.shape[0]
    d_a = k_k.shape[1]
    n_heads = d_a // HEAD_DIM
    r_w = decay_w2.shape[2]
    assert d_a == d and 4 * r_w == R_SMALL and gate_g2.shape[1] == R_G

    o_small = 3 * d_a
    o_gd = o_small + R_SMALL
    o_cb = o_gd + R_G
    o_gates = o_cb + 3 * d
    cols = dict(r=0, k=d, v=2 * d, cb=3 * d, cc=4 * d, cx=5 * d, ga=6 * d, gb=7 * d, small=8 * d,
                gd=8 * d + R_SMALL)
    n_in = 8 * d + R_SMALL + R_G
    n_in_pad = 8 * d + 512
    w_in_p = jnp.concatenate(
        [w_in[:, :, :o_small], w_in[:, :, o_cb:o_gates], w_in[:, :, o_gates:], w_in[:, :, o_small:o_cb],
         jnp.zeros((n_layers, d, n_in_pad - n_in), w_in.dtype)], axis=2).astype(BF16)
    w_up_b = w_up.astype(BF16)
    w_pa_b, w_pb_b, w_o_b = w_pa.astype(BF16), w_pb.astype(BF16), w_o.astype(BF16)
    w_down_b = w_down.astype(BF16)
    g2_b = gate_g2.astype(BF16)

    zrow = jnp.zeros((n_layers, r_w, d_a), F32)
    wdec = jnp.stack([jnp.concatenate([decay_w2[:, 0], zrow, zrow, zrow], axis=1),
                      jnp.concatenate([zrow, decay_w2[:, 1], zrow, zrow], axis=1)], axis=1)
    wicl = jnp.stack([jnp.concatenate([zrow, zrow, iclr_a2[:, 0], zrow], axis=1),
                      jnp.concatenate([zrow, zrow, zrow, iclr_a2[:, 1]], axis=1)], axis=1)

    cond8 = jnp.concatenate([c_ctx[None, :], c, jnp.zeros((8 - 1 - n_lat, d), F32)], axis=0)
    mod = _modulation(cond8, w_mod, b_mod)

    def layer(x, l, *, n_seq, seq_len, row0, grid, s0, final):
        tm = TOKEN_TILE
        z = _norm_mm(x, norm1_g[l], mod[l], w_in_p[l], tm=MM_TOKEN_TILE, tn=n_in_pad // 4, seq_len=seq_len,
                     row0=row0, sc_col=1, sh_col=0, name="in_proj")
        outs = []
        for rev in (False, True):
            dd = 1 if rev else 0
            outs.append(_wkv(z, s0[dd], wdec[l], wicl[l], decay_w0[l].reshape(2, 1, d_a),
                             iclr_a0[l].reshape(2, 1, d_a), k_k[l].reshape(1, d_a), k_a[l].reshape(1, d_a),
                             r_k[l].reshape(1, d_a), n_seq=n_seq, seq_len=seq_len, tb=WKV_BLOCK, reverse=rev,
                             col_r=cols["r"], col_k=cols["k"], col_v=cols["v"], col_s=cols["small"]))
        (yf, bonf, sf), (yb, bonb, sb) = outs
        x = _mix(x, yf, yb, bonf, bonb, z, mod[l], gn_w[l], gn_b[l], g2_b[l], conv_mix_w[l], conv_mix_b[l],
                 w_pa_b[l], w_pb_b[l], w_o_b[l], tm=tm, seq_len=seq_len, row0=row0,
                 period=GRID_W if grid else seq_len, cols=cols)
        u = _norm_mm(x, norm2_g[l], mod[l], w_up_b[l], tm=MM_TOKEN_TILE, tn=w_up.shape[2] // 4, seq_len=seq_len,
                     row0=row0, sc_col=4, sh_col=3, name="ffn_up")
        x = _ffn(x, u, mod[l], conv_ffn_w[l], conv_ffn_b[l], w_down_b[l], norm_f_g, tm=tm, seq_len=seq_len,
                 row0=row0, grid=grid, final=final)
        return x, sf, sb

    xp = x_prompt.reshape(n_ctx * ctx_len, d)
    zero_state = jnp.zeros((n_ctx, n_heads, HEAD_DIM, HEAD_DIM), F32)
    ctx_states = []
    for l in range(n_layers):
        xp, sf, sb = layer(xp, l, n_seq=n_ctx, seq_len=ctx_len, row0=0, grid=False,
                           s0=(zero_state, zero_state), final=(l == n_layers - 1))
        ctx_states.append(jnp.stack([sf, sb], axis=1))
    new_state = jnp.stack(ctx_states, axis=1).astype(x_prompt.dtype)
    y_prompt = xp.reshape(n_ctx, ctx_len, d)

    xs = x_sample.reshape(n_lat * lat_len, d)
    for l in range(n_layers):
        s0 = (state_wkv[:, l, 0].astype(F32), state_wkv[:, l, 1].astype(F32))
        xs, _, _ = layer(xs, l, n_seq=n_lat, seq_len=lat_len, row0=1, grid=True, s0=s0,
                         final=(l == n_layers - 1))
    y_sample = xs.reshape(n_lat, lat_len, d)
    return (y_prompt, y_sample, new_state)
```

```python
import functools

import jax
import jax.numpy as jnp
from jax import lax
from jax.experimental import pallas as pl
from jax.experimental.pallas import tpu as pltpu

F32 = jnp.float32
BF16 = jnp.bfloat16
HIGHEST = lax.Precision.HIGHEST

HEAD_DIM = 64
PAIR = 2 * HEAD_DIM
CHUNK = 64
WKV_PAIRS_PER_STEP = 4
WKV_BLOCK = 256
TOKEN_TILE = 256
MM_TOKEN_TILE = 1024
FFN_COL_CHUNK = 256
GRID_W = 64
SUBLANES = 8
R_SMALL = 256
R_G = 128
NORM_EPS = 1e-6
GN_EPS = 64e-5
L2_EPS = 1e-12
VMEM_LIMIT = 56 * 1024 * 1024


def _cparams():
    return pltpu.CompilerParams(vmem_limit_bytes=VMEM_LIMIT)


def _dot(a, b, hi=False):
    if hi:
        return jnp.dot(a, b, precision=HIGHEST, preferred_element_type=F32)
    return jnp.dot(a.astype(BF16), b.astype(BF16), preferred_element_type=F32)


def _dot_nt(a, b):
    return lax.dot_general(a.astype(BF16), b.astype(BF16), (((1,), (1,)), ((), ())),
                           preferred_element_type=F32)


def _dot_tn(a, b):
    return jnp.dot(a.T.astype(BF16), b.astype(BF16), preferred_element_type=F32)


def _split(x):
    hi = x.astype(BF16)
    return hi, (x - hi.astype(F32)).astype(BF16)


def _dot_split_rhs(a, b):
    hi, lo = _split(b)
    ab = a.astype(BF16)
    return jnp.dot(ab, hi, preferred_element_type=F32) + jnp.dot(ab, lo, preferred_element_type=F32)


def _sigmoid(x):
    return 1.0 / (1.0 + jnp.exp(-x))


def _silu(x):
    return x * _sigmoid(x)


def _mod_kernel(c_ref, w_ref, b_ref, o_ref):
    s = _silu(c_ref[...])
    o_ref[0] = _dot(s, w_ref[0], hi=True) + b_ref[0]


def _modulation(cond8, w_mod, b_mod):
    n_layers, d, n = w_mod.shape
    tn = n // 4
    return pl.pallas_call(
        _mod_kernel,
        grid=(n_layers, n // tn),
        in_specs=[pl.BlockSpec((8, d), lambda l, j: (0, 0)),
                  pl.BlockSpec((1, d, tn), lambda l, j: (l, 0, j)),
                  pl.BlockSpec((1, 1, tn), lambda l, j: (l, 0, j))],
        out_specs=pl.BlockSpec((1, 8, tn), lambda l, j: (l, 0, j)),
        out_shape=jax.ShapeDtypeStruct((n_layers, 8, n), F32),
        compiler_params=_cparams(),
        name="modulation",
    )(cond8, w_mod, b_mod.reshape(n_layers, 1, n))


def _mod_row(i, tm, seq_len, row0):
    if row0 == 0:
        return 0
    return row0 + (i * tm) // seq_len


def _norm_mm_kernel(x_ref, g_ref, mod_ref, w_ref, o_ref, h_ref, *, tm, seq_len, row0, sc_col, sh_col):
    i = pl.program_id(0)
    d = x_ref.shape[1]

    @pl.when(pl.program_id(1) == 0)
    def _():
        x = x_ref[...]
        y = x * lax.rsqrt(jnp.mean(x * x, axis=-1, keepdims=True) + NORM_EPS) * g_ref[...]
        row = _mod_row(i, tm, seq_len, row0)
        sc = mod_ref[pl.ds(row, 1), sc_col * d:(sc_col + 1) * d]
        sh = mod_ref[pl.ds(row, 1), sh_col * d:(sh_col + 1) * d]
        h_ref[...] = (y * (1.0 + sc) + sh).astype(BF16)

    o_ref[...] = jnp.dot(h_ref[...], w_ref[...], preferred_element_type=F32).astype(o_ref.dtype)


def _norm_mm(x, g, mod_l, w, *, tm, tn, seq_len, row0, sc_col, sh_col, name):
    t, d = x.shape
    n = w.shape[1]
    assert t % tm == 0 and n % tn == 0 and (row0 == 0 or seq_len % tm == 0)
    kern = functools.partial(_norm_mm_kernel, tm=tm, seq_len=seq_len, row0=row0, sc_col=sc_col, sh_col=sh_col)
    return pl.pallas_call(
        kern,
        grid=(t // tm, n // tn),
        in_specs=[pl.BlockSpec((tm, d), lambda i, j: (i, 0)),
                  pl.BlockSpec((1, d), lambda i, j: (0, 0)),
                  pl.BlockSpec(mod_l.shape, lambda i, j: (0, 0)),
                  pl.BlockSpec((d, tn), lambda i, j: (0, j))],
        out_specs=pl.BlockSpec((tm, tn), lambda i, j: (i, j)),
        out_shape=jax.ShapeDtypeStruct((t, n), BF16),
        scratch_shapes=[pltpu.VMEM((tm, d), BF16)],
        compiler_params=_cparams(),
        name=name,
    )(x, g.reshape(1, d), mod_l, w)


def _wkv_kernel(r_ref, k_ref, v_ref, zs_ref, wdec_ref, wicl_ref, w0_ref, a0_ref, kk_ref, ka_ref, rk_ref, s0_ref,
                y_ref, bonus_ref, sfin_ref, state_ref, *, reverse, n_chunks, n_pp):
    jt = pl.program_id(2)
    c = CHUNK
    half = n_chunks // 2
    order = list(range(n_chunks - 1, -1, -1) if reverse else range(n_chunks))
    units_a = [(pi, ci) for ci in order[:half] for pi in range(n_pp)]
    units_b = [(pi, ci) for ci in order[half:] for pi in range(n_pp)]

    @pl.when(jt == 0)
    def _():
        state_ref[...] = jnp.zeros(state_ref.shape, F32)
        for hi in range(2 * n_pp):
            lo = (hi % 2) * HEAD_DIM
            state_ref[hi // 2, lo:lo + HEAD_DIM, lo:lo + HEAD_DIM] = s0_ref[0, hi]

    lane = lax.broadcasted_iota(jnp.int32, (PAIR, PAIR), 1)
    row = lax.broadcasted_iota(jnp.int32, (PAIR, PAIR), 0)
    own = (lane // HEAD_DIM) == (row // HEAD_DIM)
    head0_row = lax.broadcasted_iota(jnp.int32, (1, PAIR), 1) < HEAD_DIM
    if reverse:
        strict, incl = lane > row, lane >= row
    else:
        strict, incl = row > lane, row >= lane
    head0 = lax.broadcasted_iota(jnp.int32, (c, PAIR), 1) < HEAD_DIM

    def head_sum(x):
        s0 = jnp.sum(jnp.where(head0_row, x, 0.0), axis=-1, keepdims=True)
        s1 = jnp.sum(jnp.where(head0_row, 0.0, x), axis=-1, keepdims=True)
        return jnp.where(head0_row, s0, s1)

    def swap_heads(x):
        return pltpu.roll(x, HEAD_DIM, 1)

    def stack(x):
        return jnp.concatenate([jnp.where(head0, x, 0.0), jnp.where(head0, 0.0, x)], axis=0)

    def chunk_operands(units, src):
        r, k_d, v, a_vec, b_vec, log_decay = src
        tri = incl[:c, :c].astype(F32)
        cut = lambda x, u: x[u[1] * c:(u[1] + 1) * c, u[0] * PAIR:(u[0] + 1) * PAIR]
        lw = [cut(log_decay, u) for u in units]
        cum = [_dot_split_rhs(tri, x) for x in lw]
        tot = [x[0:1] if reverse else x[c - 1:c] for x in cum]
        g_in = [jnp.exp(x) for x in cum]
        g_ex = [jnp.exp(x - y) for x, y in zip(cum, lw)]
        g_inv = [jnp.exp(-x) for x in cum]
        g_tot = [jnp.exp(t) for t in tot]
        g_hat = [t * x for t, x in zip(g_tot, g_inv)]
        aa = [stack(cut(a_vec, u) * g) for u, g in zip(units, g_ex)]
        ra = [stack(cut(r, u) * g) for u, g in zip(units, g_in)]
        bb = [stack((cut(b_vec, u) * g).astype(BF16)) for u, g in zip(units, g_inv)]
        kt = [stack((cut(k_d, u) * g).astype(BF16)) for u, g in zip(units, g_inv)]
        bh = [stack((cut(b_vec, u) * g).astype(BF16)) for u, g in zip(units, g_hat)]
        kh = [stack((cut(k_d, u) * g).astype(BF16)) for u, g in zip(units, g_hat)]
        vm = [stack(cut(v, u)) for u in units]
        vsw = [swap_heads(w) for w in vm]
        vm = [w.astype(BF16) for w in vm]
        return g_tot, aa, ra, bb, kt, bh, kh, vm, vsw

    def chunk_local(ops):
        g_tot, aa, ra, bb, kt, bh, kh, vm, vsw = ops
        scores = [_dot_nt(jnp.concatenate([a, b], axis=0), jnp.concatenate([e, f], axis=0))
                  for a, b, e, f in zip(aa, ra, bb, kt)]
        n_pow = [jnp.where(strict, s[:PAIR, :PAIR], 0.0) for s in scores]
        n_ak = [jnp.where(strict, s[:PAIR, PAIR:], 0.0) for s in scores]
        m_rb = [jnp.where(incl, s[PAIR:, :PAIR], 0.0) for s in scores]
        m_rk = [jnp.where(incl, s[PAIR:, PAIR:], 0.0) for s in scores]
        xs = [a + swap_heads(_dot(n, w)) for a, n, w in zip(aa, n_ak, vm)]
        steps = CHUNK.bit_length() - 1
        for si in range(steps):
            if si + 1 < steps:
                n_pow = [n.astype(BF16) for n in n_pow]
                prod = [_dot(n, jnp.concatenate([x.astype(BF16), n], axis=1)) for n, x in zip(n_pow, xs)]
                xs = [x + p[:, :PAIR] for x, p in zip(xs, prod)]
                n_pow = [p[:, PAIR:] for p in prod]
            else:
                xs = [x + _dot(n, x) for n, x in zip(n_pow, xs)]
        xv = [jnp.concatenate([x, w], axis=0) for x, w in zip(xs, vsw)]
        oc = [_dot(jnp.concatenate([mb, mk], axis=1), w) for mb, mk, w in zip(m_rb, m_rk, xv)]
        rh = [x + jnp.where(own, o, 0.0) for x, o in zip(ra, oc)]
        y_loc = [swap_heads(jnp.where(own, 0.0, o)) for o in oc]
        rq = [_dot_tn(w, jnp.concatenate([b, k_], axis=0)) for w, b, k_ in zip(xv, bh, kh)]
        pp = [jnp.where(own, q, 0.0) for q in rq]
        qq = [jnp.where(own, jnp.concatenate([q[HEAD_DIM:], q[:HEAD_DIM]], axis=0), 0.0) for q in rq]
        return g_tot, pp, qq, rh, y_loc

    zs = zs_ref[...].astype(F32)
    w_lin = w0_ref[0] + _dot(jnp.tanh(zs), wdec_ref[0])
    log_decay = -jnp.exp(-0.5) * _sigmoid(w_lin)
    a_lr = _sigmoid(a0_ref[0] + _dot(zs, wicl_ref[0]))
    r = r_ref[...].astype(F32)
    k = k_ref[...].astype(F32)
    v = v_ref[...].astype(F32)
    kk = k * kk_ref[...]
    k_d = k * (1.0 + (a_lr - 1.0) * ka_ref[...])
    kk_sq = kk * kk
    rkr = r * k_d * rk_ref[...]
    lanes = lambda x, pi: x[:, pi * PAIR:(pi + 1) * PAIR]
    kk_norm = jnp.concatenate([jnp.sqrt(head_sum(lanes(kk_sq, pi))) for pi in range(n_pp)], axis=1)
    bonus = jnp.concatenate([head_sum(lanes(rkr, pi)) for pi in range(n_pp)], axis=1) * v
    bonus_ref[0] = bonus.astype(bonus_ref.dtype)
    kk = kk / jnp.maximum(kk_norm, L2_EPS)
    src = (r, k_d, v, -kk, kk * a_lr, log_decay)

    local = {}
    for units in (units_a, units_b):
        for u, vals in zip(units, zip(*chunk_local(chunk_operands(units, src)))):
            local[u] = vals

    state = [state_ref[pi] for pi in range(n_pp)]
    ys = {}
    for ci in order:
        for pi in range(n_pp):
            g_tot, pp, qq, rh, y_loc = local[(pi, ci)]
            ym = y_loc + _dot_nt(rh, state[pi])
            state[pi] = state[pi] * g_tot + _dot(state[pi], pp) + qq
            ys[(pi, ci)] = ym[:c] + ym[c:]
    y_ref[0] = jnp.concatenate(
        [jnp.concatenate([ys[(pi, ci)] for pi in range(n_pp)], axis=1) for ci in range(n_chunks)],
        axis=0).astype(y_ref.dtype)
    for pi in range(n_pp):
        state_ref[pi] = state[pi]

    @pl.when(jt == pl.num_programs(2) - 1)
    def _():
        for hi in range(2 * n_pp):
            lo = (hi % 2) * HEAD_DIM
            sfin_ref[0, hi] = state[hi // 2][lo:lo + HEAD_DIM, lo:lo + HEAD_DIM]


def _wkv(z, s0p, wdec, wicl, w0, a0, k_k, k_a, r_k, *, n_seq, seq_len, tb, reverse, col_r, col_k, col_v, col_s):
    t = z.shape[0]
    n_pairs = k_k.shape[1] // PAIR
    nb = seq_len // tb
    d = 1 if reverse else 0
    n_pp = WKV_PAIRS_PER_STEP
    wl = n_pp * PAIR
    assert (tb // CHUNK) % 2 == 0 and n_pairs % n_pp == 0 and seq_len % tb == 0

    def tblk(b, j):
        return b * nb + ((nb - 1 - j) if reverse else j)

    kern = functools.partial(_wkv_kernel, reverse=reverse, n_chunks=tb // CHUNK, n_pp=n_pp)
    tok = lambda col: pl.BlockSpec((tb, wl), lambda b, p, j: (tblk(b, j), col // wl + p))
    vec = pl.BlockSpec((1, wl), lambda b, p, j: (0, p))
    dvec = pl.BlockSpec((1, 1, wl), lambda b, p, j: (d, 0, p))
    dmat = pl.BlockSpec((1, R_SMALL, wl), lambda b, p, j: (d, 0, p))
    st = pl.BlockSpec((1, 2 * n_pp, HEAD_DIM, HEAD_DIM), lambda b, p, j: (b, p, 0, 0))
    out_tok = pl.BlockSpec((1, tb, wl), lambda b, p, j: (0, tblk(b, j), p))
    y, bonus, sfin = pl.pallas_call(
        kern,
        grid=(n_seq, n_pairs // n_pp, nb),
        in_specs=[tok(col_r), tok(col_k), tok(col_v),
                  pl.BlockSpec((tb, R_SMALL), lambda b, p, j: (tblk(b, j), col_s // R_SMALL)),
                  dmat, dmat, dvec, dvec, vec, vec, vec, st],
        out_specs=[out_tok, out_tok, st],
        out_shape=[jax.ShapeDtypeStruct((1, t, n_pairs * PAIR), BF16),
                   jax.ShapeDtypeStruct((1, t, n_pairs * PAIR), BF16),
                   jax.ShapeDtypeStruct((n_seq, 2 * n_pairs, HEAD_DIM, HEAD_DIM), F32)],
        scratch_shapes=[pltpu.VMEM((n_pp, PAIR, PAIR), F32)],
        compiler_params=_cparams(),
        name="wkv_bwd" if reverse else "wkv_fwd",
    )(z, z, z, z, wdec, wicl, w0, a0, k_k, k_a, r_k, s0p)
    return y[0], bonus[0], sfin


def _shift_prev(x, period):
    rolled = pltpu.roll(x, 1, 0)
    if period == x.shape[0]:
        t = lax.broadcasted_iota(jnp.int32, (SUBLANES, x.shape[1]), 0)
        return jnp.concatenate([jnp.where(t == 0, 0.0, rolled[:SUBLANES]), rolled[SUBLANES:]], axis=0)
    t = lax.broadcasted_iota(jnp.int32, x.shape, 0)
    return jnp.where((t & (period - 1)) == 0, 0.0, rolled)


def _shift_next(x, period):
    n = x.shape[0]
    rolled = pltpu.roll(x, n - 1, 0)
    if period == n:
        t = lax.broadcasted_iota(jnp.int32, (SUBLANES, x.shape[1]), 0)
        return jnp.concatenate([rolled[:n - SUBLANES], jnp.where(t == SUBLANES - 1, 0.0, rolled[n - SUBLANES:])],
                               axis=0)
    t = lax.broadcasted_iota(jnp.int32, x.shape, 0)
    return jnp.where((t & (period - 1)) == period - 1, 0.0, rolled)


def _mix_kernel(x_ref, yf_ref, yb_ref, bf_ref, bb_ref, gd_ref, cb_ref, cc_ref, cx_ref, ga_ref, gb_ref, mod_ref,
                gnw_ref, gnb_ref, g2_ref, cw_ref, cbias_ref, wpa_ref, wpb_ref, wo_ref, o_ref,
                *, tm, seq_len, row0, period):
    i = pl.program_id(0)
    d = x_ref.shape[1]
    lane = lax.broadcasted_iota(jnp.int32, (PAIR, PAIR), 1)
    row = lax.broadcasted_iota(jnp.int32, (PAIR, PAIR), 0)
    head_mean = jnp.where((lane // HEAD_DIM) == (row // HEAD_DIM), 1.0 / HEAD_DIM, 0.0).astype(F32)

    y = yf_ref[...].astype(F32) + yb_ref[...].astype(F32)
    parts = []
    for s in range(d // PAIR):
        ys = y[:, s * PAIR:(s + 1) * PAIR]
        mu = _dot(ys, head_mean)
        yc = ys - mu
        var = _dot(yc * yc, head_mean)
        parts.append(yc * lax.rsqrt(var + GN_EPS))
    yn = jnp.concatenate(parts, axis=1)
    o = yn * gnw_ref[...] + gnb_ref[...] + bf_ref[...].astype(F32) + bb_ref[...].astype(F32)
    gate = _dot(_sigmoid(gd_ref[...].astype(F32)), g2_ref[...])
    o_a = o * gate

    p = cc_ref[...].astype(F32) * cx_ref[...].astype(F32)
    cw = cw_ref[...]
    conv = _shift_prev(p, period) * cw[0:1] + p * cw[1:2] + _shift_next(p, period) * cw[2:3] + cbias_ref[...]
    o_b = cb_ref[...].astype(F32) * conv

    merged = (_sigmoid(ga_ref[...].astype(F32)) * _dot(o_a, wpa_ref[...])
              + _sigmoid(gb_ref[...].astype(F32)) * _dot(o_b, wpb_ref[...]))
    mrow = _mod_row(i, tm, seq_len, row0)
    gt1 = mod_ref[pl.ds(mrow, 1), 2 * d:3 * d]
    o_ref[...] = x_ref[...] + gt1 * _dot(merged, wo_ref[...])


def _mix(x, yf, yb, bonf, bonb, z, mod_l, gn_w, gn_b, g2, conv_w, conv_b, w_pa, w_pb, w_o,
         *, tm, seq_len, row0, period, cols):
    t, d = x.shape
    kern = functools.partial(_mix_kernel, tm=tm, seq_len=seq_len, row0=row0, period=period)
    tok = pl.BlockSpec((tm, d), lambda i: (i, 0))
    zcol = lambda col: pl.BlockSpec((tm, d), lambda i: (i, col // d))
    full = lambda a: pl.BlockSpec(a.shape, lambda i: (0,) * a.ndim)
    vec = pl.BlockSpec((1, d), lambda i: (0, 0))
    return pl.pallas_call(
        kern,
        grid=(t // tm,),
        in_specs=[tok, tok, tok, tok, tok,
                  pl.BlockSpec((tm, R_G), lambda i: (i, cols["gd"] // R_G)),
                  zcol(cols["cb"]), zcol(cols["cc"]), zcol(cols["cx"]), zcol(cols["ga"]), zcol(cols["gb"]),
                  full(mod_l), vec, vec, full(g2), full(conv_w), vec, full(w_pa), full(w_pb), full(w_o)],
        out_specs=tok,
        out_shape=jax.ShapeDtypeStruct((t, d), F32),
        compiler_params=_cparams(),
        name="mix",
    )(x, yf, yb, bonf, bonb, z, z, z, z, z, z, mod_l, gn_w.reshape(1, d), gn_b.reshape(1, d), g2, conv_w,
      conv_b.reshape(1, d), w_pa, w_pb, w_o)


def _ffn_kernel(x_ref, u_ref, up_ref, un_ref, mod_ref, cw_ref, cb_ref, wd_ref, gf_ref, o_ref,
                *, tm, seq_len, row0, grid, final, kc):
    i = pl.program_id(0)
    d = x_ref.shape[1]
    d_ff = wd_ref.shape[0]
    tiles_per_seq = seq_len // tm
    first = (i % tiles_per_seq) == 0
    last = (i % tiles_per_seq) == tiles_per_seq - 1

    def conv(col):
        u = u_ref[:, col:col + kc].astype(F32)
        if grid:
            halo_p = jnp.where(first, 0.0, up_ref[:, col:col + kc].astype(F32))
            halo_n = jnp.where(last, 0.0, un_ref[:, col:col + kc].astype(F32))
            prev = jnp.concatenate([halo_p, u[:tm - GRID_W]], axis=0)
            nxt = jnp.concatenate([u[GRID_W:], halo_n], axis=0)
        else:
            prev = _shift_prev(u, seq_len)
            nxt = _shift_next(u, seq_len)
        cw = cw_ref[:, col:col + kc]
        return prev * cw[0:1] + u * cw[1:2] + nxt * cw[2:3] + cb_ref[:, col:col + kc]

    acc = jnp.zeros((tm, d), F32)
    for ci in range(d_ff // kc):
        act = conv(ci * kc)
        lin = conv(d_ff + ci * kc)
        acc = acc + _dot(_silu(act) * lin, wd_ref[ci * kc:(ci + 1) * kc, :])
    mrow = _mod_row(i, tm, seq_len, row0)
    gt2 = mod_ref[pl.ds(mrow, 1), 5 * d:6 * d]
    xn = x_ref[...] + gt2 * acc
    if final:
        xn = xn * lax.rsqrt(jnp.mean(xn * xn, axis=-1, keepdims=True) + NORM_EPS) * gf_ref[...]
    o_ref[...] = xn


def _ffn(x, u, mod_l, conv_w, conv_b, w_down, norm_f_g, *, tm, seq_len, row0, grid, final):
    t, d = x.shape
    n_up = u.shape[1]
    hb = tm // GRID_W
    n_halo = t // GRID_W
    kern = functools.partial(_ffn_kernel, tm=tm, seq_len=seq_len, row0=row0, grid=grid, final=final,
                             kc=FFN_COL_CHUNK)
    full = lambda a: pl.BlockSpec(a.shape, lambda i: (0,) * a.ndim)
    return pl.pallas_call(
        kern,
        grid=(t // tm,),
        in_specs=[pl.BlockSpec((tm, d), lambda i: (i, 0)),
                  pl.BlockSpec((tm, n_up), lambda i: (i, 0)),
                  pl.BlockSpec((GRID_W, n_up), lambda i: (jnp.maximum(i * hb - 1, 0), 0)),
                  pl.BlockSpec((GRID_W, n_up), lambda i: (jnp.minimum(i * hb + hb, n_halo - 1), 0)),
                  full(mod_l), full(conv_w), pl.BlockSpec((1, n_up), lambda i: (0, 0)), full(w_down),
                  pl.BlockSpec((1, d), lambda i: (0, 0))],
        out_specs=pl.BlockSpec((tm, d), lambda i: (i, 0)),
        out_shape=jax.ShapeDtypeStruct((t, d), F32),
        compiler_params=_cparams(),
        name="ffn",
    )(x, u, u, u, mod_l, conv_w, conv_b.reshape(1, n_up), w_down, norm_f_g.reshape(1, d))


def kernel(x_prompt, x_sample, state_wkv, c, c_ctx, w_mod, b_mod, norm1_g, w_in, decay_w0, decay_w2, iclr_a0,
           iclr_a2, gate_g2, k_k, k_a, r_k, gn_w, gn_b, conv_mix_w, conv_mix_b, w_pa, w_pb, w_o, norm2_g, w_up,
           conv_ffn_w, conv_ffn_b, w_down, norm_f_g):
    n_ctx, ctx_len, d = x_prompt.shape
    n_lat, lat_len, _ = x_sample.shape
    n_layers = w_mod.shape[0]
    d_a = k_k.shape[1]
    n_heads = d_a // HEAD_DIM
    r_w = decay_w2.shape[2]
    assert d_a == d and 4 * r_w == R_SMALL and gate_g2.shape[1] == R_G

    o_small = 3 * d_a
    o_gd = o_small + R_SMALL
    o_cb = o_gd + R_G
    o_gates = o_cb + 3 * d
    cols = dict(r=0, k=d, v=2 * d, cb=3 * d, cc=4 * d, cx=5 * d, ga=6 * d, gb=7 * d, small=8 * d,
                gd=8 * d + R_SMALL)
    n_in = 8 * d + R_SMALL + R_G
    n_in_pad = 8 * d + 512
    w_in_p = jnp.concatenate(
        [w_in[:, :, :o_small], w_in[:, :, o_cb:o_gates], w_in[:, :, o_gates:], w_in[:, :, o_small:o_cb],
         jnp.zeros((n_layers, d, n_in_pad - n_in), w_in.dtype)], axis=2).astype(BF16)
    w_up_b = w_up.astype(BF16)
    w_pa_b, w_pb_b, w_o_b = w_pa.astype(BF16), w_pb.astype(BF16), w_o.astype(BF16)
    w_down_b = w_down.astype(BF16)
    g2_b = gate_g2.astype(BF16)

    zrow = jnp.zeros((n_layers, r_w, d_a), F32)
    wdec = jnp.stack([jnp.concatenate([decay_w2[:, 0], zrow, zrow, zrow], axis=1),
                      jnp.concatenate([zrow, decay_w2[:, 1], zrow, zrow], axis=1)], axis=1)
    wicl = jnp.stack([jnp.concatenate([zrow, zrow, iclr_a2[:, 0], zrow], axis=1),
                      jnp.concatenate([zrow, zrow, zrow, iclr_a2[:, 1]], axis=1)], axis=1)

    cond8 = jnp.concatenate([c_ctx[None, :], c, jnp.zeros((8 - 1 - n_lat, d), F32)], axis=0)
    mod = _modulation(cond8, w_mod, b_mod)

    def layer(x, l, *, n_seq, seq_len, row0, grid, s0, final):
        tm = TOKEN_TILE
        z = _norm_mm(x, norm1_g[l], mod[l], w_in_p[l], tm=MM_TOKEN_TILE, tn=n_in_pad // 4, seq_len=seq_len,
                     row0=row0, sc_col=1, sh_col=0, name="in_proj")
        outs = []
        for rev in (False, True):
            dd = 1 if rev else 0
            outs.append(_wkv(z, s0[dd], wdec[l], wicl[l], decay_w0[l].reshape(2, 1, d_a),
                             iclr_a0[l].reshape(2, 1, d_a), k_k[l].reshape(1, d_a), k_a[l].reshape(1, d_a),
                             r_k[l].reshape(1, d_a), n_seq=n_seq, seq_len=seq_len, tb=WKV_BLOCK, reverse=rev,
                             col_r=cols["r"], col_k=cols["k"], col_v=cols["v"], col_s=cols["small"]))
        (yf, bonf, sf), (yb, bonb, sb) = outs
        x = _mix(x, yf, yb, bonf, bonb, z, mod[l], gn_w[l], gn_b[l], g2_b[l], conv_mix_w[l], conv_mix_b[l],
                 w_pa_b[l], w_pb_b[l], w_o_b[l], tm=tm, seq_len=seq_len, row0=row0,
                 period=GRID_W if grid else seq_len, cols=cols)
        u = _norm_mm(x, norm2_g[l], mod[l], w_up_b[l], tm=MM_TOKEN_TILE, tn=w_up.shape[2] // 4, seq_len=seq_len,
                     row0=row0, sc_col=4, sh_col=3, name="ffn_up")
        x = _ffn(x, u, mod[l], conv_ffn_w[l], conv_ffn_b[l], w_down_b[l], norm_f_g, tm=tm, seq_len=seq_len,
                 row0=row0, grid=grid, final=final)
        return x, sf, sb

    xp = x_prompt.reshape(n_ctx * ctx_len, d)
    zero_state = jnp.zeros((n_ctx, n_heads, HEAD_DIM, HEAD_DIM), F32)
    ctx_states = []
    for l in range(n_layers):
        xp, sf, sb = layer(xp, l, n_seq=n_ctx, seq_len=ctx_len, row0=0, grid=False,
                           s0=(zero_state, zero_state), final=(l == n_layers - 1))
        ctx_states.append(jnp.stack([sf, sb], axis=1))
    new_state = jnp.stack(ctx_states, axis=1).astype(x_prompt.dtype)
    y_prompt = xp.reshape(n_ctx, ctx_len, d)

    xs = x_sample.reshape(n_lat * lat_len, d)
    for l in range(n_layers):
        s0 = (state_wkv[:, l, 0].astype(F32), state_wkv[:, l, 1].astype(F32))
        xs, _, _ = layer(xs, l, n_seq=n_lat, seq_len=lat_len, row0=1, grid=True, s0=s0,
                         final=(l == n_layers - 1))
    y_sample = xs.reshape(n_lat, lat_len, d)
    return (y_prompt, y_sample, new_state)
```

```python
import functools

import jax
import jax.numpy as jnp
from jax import lax
from jax.experimental import pallas as pl
from jax.experimental.pallas import tpu as pltpu

F32 = jnp.float32
BF16 = jnp.bfloat16
HIGHEST = lax.Precision.HIGHEST

HEAD_DIM = 64
PAIR = 2 * HEAD_DIM
CHUNK = 64
WKV_PAIRS_PER_STEP = 8
WKV_WAVES = 2
WKV_BLOCK = 256
TOKEN_TILE = 256
MM_TOKEN_TILE = 1024
FFN_COL_CHUNK = 256
GRID_W = 64
SUBLANES = 8
R_SMALL = 256
R_G = 128
NORM_EPS = 1e-6
GN_EPS = 64e-5
L2_EPS = 1e-12
VMEM_LIMIT = 56 * 1024 * 1024


def _cparams():
    return pltpu.CompilerParams(vmem_limit_bytes=VMEM_LIMIT)


def _dot(a, b, hi=False):
    if hi:
        return jnp.dot(a, b, precision=HIGHEST, preferred_element_type=F32)
    return jnp.dot(a.astype(BF16), b.astype(BF16), preferred_element_type=F32)


def _dot_nt(a, b):
    return lax.dot_general(a.astype(BF16), b.astype(BF16), (((1,), (1,)), ((), ())),
                           preferred_element_type=F32)


def _dot_tn(a, b):
    return jnp.dot(a.T.astype(BF16), b.astype(BF16), preferred_element_type=F32)


def _split(x):
    hi = x.astype(BF16)
    return hi, (x - hi.astype(F32)).astype(BF16)


def _dot_split_rhs(a, b):
    hi, lo = _split(b)
    ab = a.astype(BF16)
    return jnp.dot(ab, hi, preferred_element_type=F32) + jnp.dot(ab, lo, preferred_element_type=F32)


def _sigmoid(x):
    return 1.0 / (1.0 + jnp.exp(-x))


def _silu(x):
    return x * _sigmoid(x)


def _mod_kernel(c_ref, w_ref, b_ref, o_ref):
    s = _silu(c_ref[...])
    o_ref[0] = _dot(s, w_ref[0], hi=True) + b_ref[0]


def _modulation(cond8, w_mod, b_mod):
    n_layers, d, n = w_mod.shape
    tn = n // 4
    return pl.pallas_call(
        _mod_kernel,
        grid=(n_layers, n // tn),
        in_specs=[pl.BlockSpec((8, d), lambda l, j: (0, 0)),
                  pl.BlockSpec((1, d, tn), lambda l, j: (l, 0, j)),
                  pl.BlockSpec((1, 1, tn), lambda l, j: (l, 0, j))],
        out_specs=pl.BlockSpec((1, 8, tn), lambda l, j: (l, 0, j)),
        out_shape=jax.ShapeDtypeStruct((n_layers, 8, n), F32),
        compiler_params=_cparams(),
        name="modulation",
    )(cond8, w_mod, b_mod.reshape(n_layers, 1, n))


def _mod_row(i, tm, seq_len, row0):
    if row0 == 0:
        return 0
    return row0 + (i * tm) // seq_len


def _norm_mm_kernel(x_ref, g_ref, mod_ref, w_ref, o_ref, h_ref, *, tm, seq_len, row0, sc_col, sh_col):
    i = pl.program_id(0)
    d = x_ref.shape[1]

    @pl.when(pl.program_id(1) == 0)
    def _():
        x = x_ref[...]
        y = x * lax.rsqrt(jnp.mean(x * x, axis=-1, keepdims=True) + NORM_EPS) * g_ref[...]
        row = _mod_row(i, tm, seq_len, row0)
        sc = mod_ref[pl.ds(row, 1), sc_col * d:(sc_col + 1) * d]
        sh = mod_ref[pl.ds(row, 1), sh_col * d:(sh_col + 1) * d]
        h_ref[...] = (y * (1.0 + sc) + sh).astype(BF16)

    o_ref[...] = jnp.dot(h_ref[...], w_ref[...], preferred_element_type=F32).astype(o_ref.dtype)


def _norm_mm(x, g, mod_l, w, layer, *, tm, tn, seq_len, row0, sc_col, sh_col, name):
    t, d = x.shape
    n = w.shape[2]
    assert t % tm == 0 and n % tn == 0 and (row0 == 0 or seq_len % tm == 0)
    kern = functools.partial(_norm_mm_kernel, tm=tm, seq_len=seq_len, row0=row0, sc_col=sc_col, sh_col=sh_col)
    return pl.pallas_call(
        kern,
        grid=(t // tm, n // tn),
        in_specs=[pl.BlockSpec((tm, d), lambda i, j: (i, 0)),
                  pl.BlockSpec((1, d), lambda i, j: (0, 0)),
                  pl.BlockSpec(mod_l.shape, lambda i, j: (0, 0)),
                  pl.BlockSpec((None, d, tn), lambda i, j: (layer, 0, j))],
        out_specs=pl.BlockSpec((tm, tn), lambda i, j: (i, j)),
        out_shape=jax.ShapeDtypeStruct((t, n), BF16),
        scratch_shapes=[pltpu.VMEM((tm, d), BF16)],
        compiler_params=_cparams(),
        name=name,
    )(x, g.reshape(1, d), mod_l, w)


def _wkv_kernel(r_ref, k_ref, v_ref, zs_ref, wdec_ref, wicl_ref, w0_ref, a0_ref, kk_ref, ka_ref, rk_ref, s0_ref,
                y_ref, bonus_ref, sfin_ref, state_ref, *, reverse, n_chunks, n_pp):
    jt = pl.program_id(2)
    c = CHUNK
    order = list(range(n_chunks - 1, -1, -1) if reverse else range(n_chunks))
    per_wave = n_chunks // WKV_WAVES
    waves = [[(pi, ci) for ci in order[w * per_wave:(w + 1) * per_wave] for pi in range(n_pp)]
             for w in range(WKV_WAVES)]

    @pl.when(jt == 0)
    def _():
        state_ref[...] = jnp.zeros(state_ref.shape, F32)
        for hi in range(2 * n_pp):
            lo = (hi % 2) * HEAD_DIM
            state_ref[hi // 2, lo:lo + HEAD_DIM, lo:lo + HEAD_DIM] = s0_ref[0, hi]

    lane = lax.broadcasted_iota(jnp.int32, (PAIR, PAIR), 1)
    row = lax.broadcasted_iota(jnp.int32, (PAIR, PAIR), 0)
    own = (lane // HEAD_DIM) == (row // HEAD_DIM)
    head0_row = lax.broadcasted_iota(jnp.int32, (1, PAIR), 1) < HEAD_DIM
    if reverse:
        strict, incl = lane > row, lane >= row
    else:
        strict, incl = row > lane, row >= lane
    head0 = lax.broadcasted_iota(jnp.int32, (c, PAIR), 1) < HEAD_DIM

    def head_sum(x):
        s0 = jnp.sum(jnp.where(head0_row, x, 0.0), axis=-1, keepdims=True)
        s1 = jnp.sum(jnp.where(head0_row, 0.0, x), axis=-1, keepdims=True)
        return jnp.where(head0_row, s0, s1)

    def swap_heads(x):
        return pltpu.roll(x, HEAD_DIM, 1)

    def stack(x):
        return jnp.concatenate([jnp.where(head0, x, 0.0), jnp.where(head0, 0.0, x)], axis=0)

    def chunk_operands(units, src):
        r, k_d, v, a_vec, b_vec, log_decay = src
        tri = incl[:c, :c].astype(F32)
        cut = lambda x, u: x[u[1] * c:(u[1] + 1) * c, u[0] * PAIR:(u[0] + 1) * PAIR]
        lw = [cut(log_decay, u) for u in units]
        cum = [_dot_split_rhs(tri, x) for x in lw]
        tot = [x[0:1] if reverse else x[c - 1:c] for x in cum]
        g_in = [jnp.exp(x) for x in cum]
        g_ex = [jnp.exp(x - y) for x, y in zip(cum, lw)]
        g_inv = [jnp.exp(-x) for x in cum]
        g_tot = [jnp.exp(t) for t in tot]
        g_hat = [t * x for t, x in zip(g_tot, g_inv)]
        aa = [stack(cut(a_vec, u) * g) for u, g in zip(units, g_ex)]
        ra = [stack(cut(r, u) * g) for u, g in zip(units, g_in)]
        bb = [stack((cut(b_vec, u) * g).astype(BF16)) for u, g in zip(units, g_inv)]
        kt = [stack((cut(k_d, u) * g).astype(BF16)) for u, g in zip(units, g_inv)]
        bh = [stack((cut(b_vec, u) * g).astype(BF16)) for u, g in zip(units, g_hat)]
        kh = [stack((cut(k_d, u) * g).astype(BF16)) for u, g in zip(units, g_hat)]
        vm = [stack(cut(v, u)) for u in units]
        vsw = [swap_heads(w) for w in vm]
        vm = [w.astype(BF16) for w in vm]
        return g_tot, aa, ra, bb, kt, bh, kh, vm, vsw

    def chunk_local(ops):
        g_tot, aa, ra, bb, kt, bh, kh, vm, vsw = ops
        scores = [_dot_nt(jnp.concatenate([a, b], axis=0), jnp.concatenate([e, f], axis=0))
                  for a, b, e, f in zip(aa, ra, bb, kt)]
        n_pow = [jnp.where(strict, s[:PAIR, :PAIR], 0.0) for s in scores]
        n_ak = [jnp.where(strict, s[:PAIR, PAIR:], 0.0) for s in scores]
        m_rb = [jnp.where(incl, s[PAIR:, :PAIR], 0.0) for s in scores]
        m_rk = [jnp.where(incl, s[PAIR:, PAIR:], 0.0) for s in scores]
        xs = [a + swap_heads(_dot(n, w)) for a, n, w in zip(aa, n_ak, vm)]
        steps = CHUNK.bit_length() - 1
        for si in range(steps):
            if si + 1 < steps:
                n_pow = [n.astype(BF16) for n in n_pow]
                prod = [_dot(n, jnp.concatenate([x.astype(BF16), n], axis=1)) for n, x in zip(n_pow, xs)]
                xs = [x + p[:, :PAIR] for x, p in zip(xs, prod)]
                n_pow = [p[:, PAIR:] for p in prod]
            else:
                xs = [x + _dot(n, x) for n, x in zip(n_pow, xs)]
        xv = [jnp.concatenate([x, w], axis=0) for x, w in zip(xs, vsw)]
        oc = [_dot(jnp.concatenate([mb, mk], axis=1), w) for mb, mk, w in zip(m_rb, m_rk, xv)]
        rh = [x + jnp.where(own, o, 0.0) for x, o in zip(ra, oc)]
        y_loc = [swap_heads(jnp.where(own, 0.0, o)) for o in oc]
        rq = [_dot_tn(w, jnp.concatenate([b, k_], axis=0)) for w, b, k_ in zip(xv, bh, kh)]
        pp = [jnp.where(own, q, 0.0) for q in rq]
        qq = [jnp.where(own, jnp.concatenate([q[HEAD_DIM:], q[:HEAD_DIM]], axis=0), 0.0) for q in rq]
        return g_tot, pp, qq, rh, y_loc

    zs = zs_ref[...].astype(F32)
    w_lin = w0_ref[0] + _dot(jnp.tanh(zs), wdec_ref[0])
    log_decay = -jnp.exp(-0.5) * _sigmoid(w_lin)
    a_lr = _sigmoid(a0_ref[0] + _dot(zs, wicl_ref[0]))
    r = r_ref[...].astype(F32)
    k = k_ref[...].astype(F32)
    v = v_ref[...].astype(F32)
    kk = k * kk_ref[...]
    k_d = k * (1.0 + (a_lr - 1.0) * ka_ref[...])
    kk_sq = kk * kk
    rkr = r * k_d * rk_ref[...]
    lanes = lambda x, pi: x[:, pi * PAIR:(pi + 1) * PAIR]
    kk_norm = jnp.concatenate([jnp.sqrt(head_sum(lanes(kk_sq, pi))) for pi in range(n_pp)], axis=1)
    bonus = jnp.concatenate([head_sum(lanes(rkr, pi)) for pi in range(n_pp)], axis=1) * v
    bonus_ref[0] = bonus.astype(bonus_ref.dtype)
    kk = kk / jnp.maximum(kk_norm, L2_EPS)
    src = (r, k_d, v, -kk, kk * a_lr, log_decay)

    local = {}
    for units in waves:
        for u, vals in zip(units, zip(*chunk_local(chunk_operands(units, src)))):
            local[u] = vals

    state = [state_ref[pi] for pi in range(n_pp)]
    ys = {}
    for ci in order:
        for pi in range(n_pp):
            g_tot, pp, qq, rh, y_loc = local[(pi, ci)]
            ym = y_loc + _dot_nt(rh, state[pi])
            state[pi] = state[pi] * g_tot + _dot(state[pi], pp) + qq
            ys[(pi, ci)] = ym[:c] + ym[c:]
    y_ref[0] = jnp.concatenate(
        [jnp.concatenate([ys[(pi, ci)] for pi in range(n_pp)], axis=1) for ci in range(n_chunks)],
        axis=0).astype(y_ref.dtype)
    for pi in range(n_pp):
        state_ref[pi] = state[pi]

    @pl.when(jt == pl.num_programs(2) - 1)
    def _():
        for hi in range(2 * n_pp):
            lo = (hi % 2) * HEAD_DIM
            sfin_ref[0, hi] = state[hi // 2][lo:lo + HEAD_DIM, lo:lo + HEAD_DIM]


def _wkv(z, s0p, wdec, wicl, layer, w0, a0, k_k, k_a, r_k, *, n_seq, seq_len, tb, reverse, col_r, col_k, col_v,
         col_s):
    t = z.shape[0]
    n_pairs = k_k.shape[1] // PAIR
    nb = seq_len // tb
    d = 1 if reverse else 0
    n_pp = WKV_PAIRS_PER_STEP
    wl = n_pp * PAIR
    assert (tb // CHUNK) % 2 == 0 and n_pairs % n_pp == 0 and seq_len % tb == 0

    def tblk(b, j):
        return b * nb + ((nb - 1 - j) if reverse else j)

    kern = functools.partial(_wkv_kernel, reverse=reverse, n_chunks=tb // CHUNK, n_pp=n_pp)
    tok = lambda col: pl.BlockSpec((tb, wl), lambda b, p, j: (tblk(b, j), col // wl + p))
    vec = pl.BlockSpec((1, wl), lambda b, p, j: (0, p))
    dvec = pl.BlockSpec((1, 1, wl), lambda b, p, j: (d, 0, p))
    dmat = pl.BlockSpec((None, 1, R_SMALL, wl), lambda b, p, j: (layer, d, 0, p))
    st = pl.BlockSpec((1, 2 * n_pp, HEAD_DIM, HEAD_DIM), lambda b, p, j: (b, p, 0, 0))
    out_tok = pl.BlockSpec((1, tb, wl), lambda b, p, j: (0, tblk(b, j), p))
    y, bonus, sfin = pl.pallas_call(
        kern,
        grid=(n_seq, n_pairs // n_pp, nb),
        in_specs=[tok(col_r), tok(col_k), tok(col_v),
                  pl.BlockSpec((tb, R_SMALL), lambda b, p, j: (tblk(b, j), col_s // R_SMALL)),
                  dmat, dmat, dvec, dvec, vec, vec, vec, st],
        out_specs=[out_tok, out_tok, st],
        out_shape=[jax.ShapeDtypeStruct((1, t, n_pairs * PAIR), BF16),
                   jax.ShapeDtypeStruct((1, t, n_pairs * PAIR), BF16),
                   jax.ShapeDtypeStruct((n_seq, 2 * n_pairs, HEAD_DIM, HEAD_DIM), F32)],
        scratch_shapes=[pltpu.VMEM((n_pp, PAIR, PAIR), F32)],
        compiler_params=_cparams(),
        name="wkv_bwd" if reverse else "wkv_fwd",
    )(z, z, z, z, wdec, wicl, w0, a0, k_k, k_a, r_k, s0p)
    return y[0], bonus[0], sfin


def _shift_prev(x, period):
    rolled = pltpu.roll(x, 1, 0)
    if period == x.shape[0]:
        t = lax.broadcasted_iota(jnp.int32, (SUBLANES, x.shape[1]), 0)
        return jnp.concatenate([jnp.where(t == 0, 0.0, rolled[:SUBLANES]), rolled[SUBLANES:]], axis=0)
    t = lax.broadcasted_iota(jnp.int32, x.shape, 0)
    return jnp.where((t & (period - 1)) == 0, 0.0, rolled)


def _shift_next(x, period):
    n = x.shape[0]
    rolled = pltpu.roll(x, n - 1, 0)
    if period == n:
        t = lax.broadcasted_iota(jnp.int32, (SUBLANES, x.shape[1]), 0)
        return jnp.concatenate([rolled[:n - SUBLANES], jnp.where(t == SUBLANES - 1, 0.0, rolled[n - SUBLANES:])],
                               axis=0)
    t = lax.broadcasted_iota(jnp.int32, x.shape, 0)
    return jnp.where((t & (period - 1)) == period - 1, 0.0, rolled)


def _mix_kernel(x_ref, yf_ref, yb_ref, bf_ref, bb_ref, gd_ref, cb_ref, cc_ref, cx_ref, ga_ref, gb_ref, mod_ref,
                gnw_ref, gnb_ref, g2_ref, cw_ref, cbias_ref, wpa_ref, wpb_ref, wo_ref, o_ref,
                *, tm, seq_len, row0, period):
    i = pl.program_id(0)
    d = x_ref.shape[1]
    lane = lax.broadcasted_iota(jnp.int32, (PAIR, PAIR), 1)
    row = lax.broadcasted_iota(jnp.int32, (PAIR, PAIR), 0)
    head_mean = jnp.where((lane // HEAD_DIM) == (row // HEAD_DIM), 1.0 / HEAD_DIM, 0.0).astype(F32)

    y = yf_ref[...].astype(F32) + yb_ref[...].astype(F32)
    parts = []
    for s in range(d // PAIR):
        ys = y[:, s * PAIR:(s + 1) * PAIR]
        mu = _dot(ys, head_mean)
        yc = ys - mu
        var = _dot(yc * yc, head_mean)
        parts.append(yc * lax.rsqrt(var + GN_EPS))
    yn = jnp.concatenate(parts, axis=1)
    o = yn * gnw_ref[...] + gnb_ref[...] + bf_ref[...].astype(F32) + bb_ref[...].astype(F32)
    gate = _dot(_sigmoid(gd_ref[...].astype(F32)), g2_ref[...])
    o_a = o * gate

    p = cc_ref[...].astype(F32) * cx_ref[...].astype(F32)
    cw = cw_ref[...]
    conv = _shift_prev(p, period) * cw[0:1] + p * cw[1:2] + _shift_next(p, period) * cw[2:3] + cbias_ref[...]
    o_b = cb_ref[...].astype(F32) * conv

    merged = (_sigmoid(ga_ref[...].astype(F32)) * _dot(o_a, wpa_ref[...])
              + _sigmoid(gb_ref[...].astype(F32)) * _dot(o_b, wpb_ref[...]))
    mrow = _mod_row(i, tm, seq_len, row0)
    gt1 = mod_ref[pl.ds(mrow, 1), 2 * d:3 * d]
    o_ref[...] = x_ref[...] + gt1 * _dot(merged, wo_ref[...])


def _mix(x, yf, yb, bonf, bonb, z, mod_l, gn_w, gn_b, g2, conv_w, conv_b, w_pa, w_pb, w_o, layer,
         *, tm, seq_len, row0, period, cols):
    t, d = x.shape
    stacked = lambda a: pl.BlockSpec((None,) + a.shape[1:], lambda i: (layer,) + (0,) * (a.ndim - 1))
    kern = functools.partial(_mix_kernel, tm=tm, seq_len=seq_len, row0=row0, period=period)
    tok = pl.BlockSpec((tm, d), lambda i: (i, 0))
    zcol = lambda col: pl.BlockSpec((tm, d), lambda i: (i, col // d))
    full = lambda a: pl.BlockSpec(a.shape, lambda i: (0,) * a.ndim)
    vec = pl.BlockSpec((1, d), lambda i: (0, 0))
    return pl.pallas_call(
        kern,
        grid=(t // tm,),
        in_specs=[tok, tok, tok, tok, tok,
                  pl.BlockSpec((tm, R_G), lambda i: (i, cols["gd"] // R_G)),
                  zcol(cols["cb"]), zcol(cols["cc"]), zcol(cols["cx"]), zcol(cols["ga"]), zcol(cols["gb"]),
                  full(mod_l), vec, vec, stacked(g2), full(conv_w), vec, stacked(w_pa), stacked(w_pb),
                  stacked(w_o)],
        out_specs=tok,
        out_shape=jax.ShapeDtypeStruct((t, d), F32),
        compiler_params=_cparams(),
        name="mix",
    )(x, yf, yb, bonf, bonb, z, z, z, z, z, z, mod_l, gn_w.reshape(1, d), gn_b.reshape(1, d), g2, conv_w,
      conv_b.reshape(1, d), w_pa, w_pb, w_o)


def _ffn_kernel(x_ref, u_ref, up_ref, un_ref, mod_ref, cw_ref, cb_ref, wd_ref, gf_ref, o_ref,
                *, tm, seq_len, row0, grid, final, kc):
    i = pl.program_id(0)
    d = x_ref.shape[1]
    d_ff = wd_ref.shape[0]
    tiles_per_seq = seq_len // tm
    first = (i % tiles_per_seq) == 0
    last = (i % tiles_per_seq) == tiles_per_seq - 1

    def conv(col):
        u = u_ref[:, col:col + kc].astype(F32)
        if grid:
            halo_p = jnp.where(first, 0.0, up_ref[:, col:col + kc].astype(F32))
            halo_n = jnp.where(last, 0.0, un_ref[:, col:col + kc].astype(F32))
            prev = jnp.concatenate([halo_p, u[:tm - GRID_W]], axis=0)
            nxt = jnp.concatenate([u[GRID_W:], halo_n], axis=0)
        else:
            prev = _shift_prev(u, seq_len)
            nxt = _shift_next(u, seq_len)
        cw = cw_ref[:, col:col + kc]
        return prev * cw[0:1] + u * cw[1:2] + nxt * cw[2:3] + cb_ref[:, col:col + kc]

    acc = jnp.zeros((tm, d), F32)
    for ci in range(d_ff // kc):
        act = conv(ci * kc)
        lin = conv(d_ff + ci * kc)
        acc = acc + _dot(_silu(act) * lin, wd_ref[ci * kc:(ci + 1) * kc, :])
    mrow = _mod_row(i, tm, seq_len, row0)
    gt2 = mod_ref[pl.ds(mrow, 1), 5 * d:6 * d]
    xn = x_ref[...] + gt2 * acc
    if final:
        xn = xn * lax.rsqrt(jnp.mean(xn * xn, axis=-1, keepdims=True) + NORM_EPS) * gf_ref[...]
    o_ref[...] = xn


def _ffn(x, u, mod_l, conv_w, conv_b, w_down, layer, norm_f_g, *, tm, seq_len, row0, grid, final):
    t, d = x.shape
    n_up = u.shape[1]
    hb = tm // GRID_W
    n_halo = t // GRID_W
    kern = functools.partial(_ffn_kernel, tm=tm, seq_len=seq_len, row0=row0, grid=grid, final=final,
                             kc=FFN_COL_CHUNK)
    full = lambda a: pl.BlockSpec(a.shape, lambda i: (0,) * a.ndim)
    return pl.pallas_call(
        kern,
        grid=(t // tm,),
        in_specs=[pl.BlockSpec((tm, d), lambda i: (i, 0)),
                  pl.BlockSpec((tm, n_up), lambda i: (i, 0)),
                  pl.BlockSpec((GRID_W, n_up), lambda i: (jnp.maximum(i * hb - 1, 0), 0)),
                  pl.BlockSpec((GRID_W, n_up), lambda i: (jnp.minimum(i * hb + hb, n_halo - 1), 0)),
                  full(mod_l), full(conv_w), pl.BlockSpec((1, n_up), lambda i: (0, 0)),
                  pl.BlockSpec((None,) + w_down.shape[1:], lambda i: (layer, 0, 0)),
                  pl.BlockSpec((1, d), lambda i: (0, 0))],
        out_specs=pl.BlockSpec((tm, d), lambda i: (i, 0)),
        out_shape=jax.ShapeDtypeStruct((t, d), F32),
        compiler_params=_cparams(),
        name="ffn",
    )(x, u, u, u, mod_l, conv_w, conv_b.reshape(1, n_up), w_down, norm_f_g.reshape(1, d))


def kernel(x_prompt, x_sample, state_wkv, c, c_ctx, w_mod, b_mod, norm1_g, w_in, decay_w0, decay_w2, iclr_a0,
           iclr_a2, gate_g2, k_k, k_a, r_k, gn_w, gn_b, conv_mix_w, conv_mix_b, w_pa, w_pb, w_o, norm2_g, w_up,
           conv_ffn_w, conv_ffn_b, w_down, norm_f_g):
    n_ctx, ctx_len, d = x_prompt.shape
    n_lat, lat_len, _ = x_sample.shape
    n_layers = w_mod.shape[0]
    d_a = k_k.shape[1]
    n_heads = d_a // HEAD_DIM
    r_w = decay_w2.shape[2]
    assert d_a == d and 4 * r_w == R_SMALL and gate_g2.shape[1] == R_G

    o_small = 3 * d_a
    o_gd = o_small + R_SMALL
    o_cb = o_gd + R_G
    o_gates = o_cb + 3 * d
    cols = dict(r=0, k=d, v=2 * d, cb=3 * d, cc=4 * d, cx=5 * d, ga=6 * d, gb=7 * d, small=8 * d,
                gd=8 * d + R_SMALL)
    n_in = 8 * d + R_SMALL + R_G
    n_in_pad = 8 * d + 512
    w_in_p = jnp.concatenate(
        [w_in[:, :, :o_small], w_in[:, :, o_cb:o_gates], w_in[:, :, o_gates:], w_in[:, :, o_small:o_cb],
         jnp.zeros((n_layers, d, n_in_pad - n_in), w_in.dtype)], axis=2).astype(BF16)
    w_up_b = w_up.astype(BF16)
    w_pa_b, w_pb_b, w_o_b = w_pa.astype(BF16), w_pb.astype(BF16), w_o.astype(BF16)
    w_down_b = w_down.astype(BF16)
    g2_b = gate_g2.astype(BF16)

    zrow = jnp.zeros((n_layers, r_w, d_a), F32)
    wdec = jnp.stack([jnp.concatenate([decay_w2[:, 0], zrow, zrow, zrow], axis=1),
                      jnp.concatenate([zrow, decay_w2[:, 1], zrow, zrow], axis=1)], axis=1)
    wicl = jnp.stack([jnp.concatenate([zrow, zrow, iclr_a2[:, 0], zrow], axis=1),
                      jnp.concatenate([zrow, zrow, zrow, iclr_a2[:, 1]], axis=1)], axis=1)

    cond8 = jnp.concatenate([c_ctx[None, :], c, jnp.zeros((8 - 1 - n_lat, d), F32)], axis=0)
    mod = _modulation(cond8, w_mod, b_mod)

    def layer(x, l, *, n_seq, seq_len, row0, grid, s0, final):
        tm = TOKEN_TILE
        z = _norm_mm(x, norm1_g[l], mod[l], w_in_p, l, tm=MM_TOKEN_TILE, tn=n_in_pad // 4, seq_len=seq_len,
                     row0=row0, sc_col=1, sh_col=0, name="in_proj")
        outs = []
        for rev in (False, True):
            dd = 1 if rev else 0
            outs.append(_wkv(z, s0[dd], wdec, wicl, l, decay_w0[l].reshape(2, 1, d_a),
                             iclr_a0[l].reshape(2, 1, d_a), k_k[l].reshape(1, d_a), k_a[l].reshape(1, d_a),
                             r_k[l].reshape(1, d_a), n_seq=n_seq, seq_len=seq_len, tb=WKV_BLOCK, reverse=rev,
                             col_r=cols["r"], col_k=cols["k"], col_v=cols["v"], col_s=cols["small"]))
        (yf, bonf, sf), (yb, bonb, sb) = outs
        x = _mix(x, yf, yb, bonf, bonb, z, mod[l], gn_w[l], gn_b[l], g2_b, conv_mix_w[l], conv_mix_b[l],
                 w_pa_b, w_pb_b, w_o_b, l, tm=tm, seq_len=seq_len, row0=row0,
                 period=GRID_W if grid else seq_len, cols=cols)
        u = _norm_mm(x, norm2_g[l], mod[l], w_up_b, l, tm=MM_TOKEN_TILE, tn=w_up.shape[2] // 4, seq_len=seq_len,
                     row0=row0, sc_col=4, sh_col=3, name="ffn_up")
        x = _ffn(x, u, mod[l], conv_ffn_w[l], conv_ffn_b[l], w_down_b, l, norm_f_g, tm=tm, seq_len=seq_len,
                 row0=row0, grid=grid, final=final)
        return x, sf, sb

    xp = x_prompt.reshape(n_ctx * ctx_len, d)
    zero_state = jnp.zeros((n_ctx, n_heads, HEAD_DIM, HEAD_DIM), F32)
    ctx_states = []
    for l in range(n_layers):
        xp, sf, sb = layer(xp, l, n_seq=n_ctx, seq_len=ctx_len, row0=0, grid=False,
                           s0=(zero_state, zero_state), final=(l == n_layers - 1))
        ctx_states.append(jnp.stack([sf, sb], axis=1))
    new_state = jnp.stack(ctx_states, axis=1).astype(x_prompt.dtype)
    y_prompt = xp.reshape(n_ctx, ctx_len, d)

    xs = x_sample.reshape(n_lat * lat_len, d)
    for l in range(n_layers):
        s0 = (state_wkv[:, l, 0].astype(F32), state_wkv[:, l, 1].astype(F32))
        xs, _, _ = layer(xs, l, n_seq=n_lat, seq_len=lat_len, row0=1, grid=True, s0=s0,
                         final=(l == n_layers - 1))
    y_sample = xs.reshape(n_lat, lat_len, d)
    return (y_prompt, y_sample, new_state)
```

```python
import functools

import jax
import jax.numpy as jnp
from jax import lax
from jax.experimental import pallas as pl
from jax.experimental.pallas import tpu as pltpu

F32 = jnp.float32
BF16 = jnp.bfloat16
HIGHEST = lax.Precision.HIGHEST

HEAD_DIM = 64
PAIR = 2 * HEAD_DIM
CHUNK = 64
WKV_PAIRS_PER_STEP = 8
WKV_WAVES = 2
WKV_BLOCK = 256
TOKEN_TILE = 512
MM_TOKEN_TILE = 1024
FFN_COL_CHUNK = 256
GRID_W = 64
SUBLANES = 8
R_SMALL = 256
R_G = 128
NORM_EPS = 1e-6
GN_EPS = 64e-5
L2_EPS = 1e-12
VMEM_LIMIT = 56 * 1024 * 1024


def _cparams():
    return pltpu.CompilerParams(vmem_limit_bytes=VMEM_LIMIT)


def _dot(a, b, hi=False):
    if hi:
        return jnp.dot(a, b, precision=HIGHEST, preferred_element_type=F32)
    return jnp.dot(a.astype(BF16), b.astype(BF16), preferred_element_type=F32)


def _dot_nt(a, b):
    return lax.dot_general(a.astype(BF16), b.astype(BF16), (((1,), (1,)), ((), ())),
                           preferred_element_type=F32)


def _dot_tn(a, b):
    return jnp.dot(a.T.astype(BF16), b.astype(BF16), preferred_element_type=F32)


def _split(x):
    hi = x.astype(BF16)
    return hi, (x - hi.astype(F32)).astype(BF16)


def _dot_split_rhs(a, b):
    hi, lo = _split(b)
    ab = a.astype(BF16)
    return jnp.dot(ab, hi, preferred_element_type=F32) + jnp.dot(ab, lo, preferred_element_type=F32)


def _sigmoid(x):
    return 1.0 / (1.0 + jnp.exp(-x))


def _silu(x):
    return x * _sigmoid(x)


def _mod_kernel(c_ref, w_ref, b_ref, o_ref):
    s = _silu(c_ref[...])
    o_ref[0] = _dot(s, w_ref[0], hi=True) + b_ref[0]


def _modulation(cond8, w_mod, b_mod):
    n_layers, d, n = w_mod.shape
    tn = n // 4
    return pl.pallas_call(
        _mod_kernel,
        grid=(n_layers, n // tn),
        in_specs=[pl.BlockSpec((8, d), lambda l, j: (0, 0)),
                  pl.BlockSpec((1, d, tn), lambda l, j: (l, 0, j)),
                  pl.BlockSpec((1, 1, tn), lambda l, j: (l, 0, j))],
        out_specs=pl.BlockSpec((1, 8, tn), lambda l, j: (l, 0, j)),
        out_shape=jax.ShapeDtypeStruct((n_layers, 8, n), F32),
        compiler_params=_cparams(),
        name="modulation",
    )(cond8, w_mod, b_mod.reshape(n_layers, 1, n))


def _mod_row(i, tm, seq_len, row0):
    if row0 == 0:
        return 0
    return row0 + (i * tm) // seq_len


def _norm_mm_kernel(x_ref, g_ref, mod_ref, w_ref, o_ref, h_ref, *, tm, seq_len, row0, sc_col, sh_col):
    i = pl.program_id(0)
    d = x_ref.shape[1]

    @pl.when(pl.program_id(1) == 0)
    def _():
        x = x_ref[...]
        y = x * lax.rsqrt(jnp.mean(x * x, axis=-1, keepdims=True) + NORM_EPS) * g_ref[...]
        row = _mod_row(i, tm, seq_len, row0)
        sc = mod_ref[pl.ds(row, 1), sc_col * d:(sc_col + 1) * d]
        sh = mod_ref[pl.ds(row, 1), sh_col * d:(sh_col + 1) * d]
        h_ref[...] = (y * (1.0 + sc) + sh).astype(BF16)

    o_ref[...] = jnp.dot(h_ref[...], w_ref[...], preferred_element_type=F32).astype(o_ref.dtype)


def _norm_mm(x, g, mod_l, w, layer, *, tm, tn, seq_len, row0, sc_col, sh_col, name):
    t, d = x.shape
    n = w.shape[2]
    assert t % tm == 0 and n % tn == 0 and (row0 == 0 or seq_len % tm == 0)
    kern = functools.partial(_norm_mm_kernel, tm=tm, seq_len=seq_len, row0=row0, sc_col=sc_col, sh_col=sh_col)
    return pl.pallas_call(
        kern,
        grid=(t // tm, n // tn),
        in_specs=[pl.BlockSpec((tm, d), lambda i, j: (i, 0)),
                  pl.BlockSpec((1, d), lambda i, j: (0, 0)),
                  pl.BlockSpec(mod_l.shape, lambda i, j: (0, 0)),
                  pl.BlockSpec((None, d, tn), lambda i, j: (layer, 0, j))],
        out_specs=pl.BlockSpec((tm, tn), lambda i, j: (i, j)),
        out_shape=jax.ShapeDtypeStruct((t, n), BF16),
        scratch_shapes=[pltpu.VMEM((tm, d), BF16)],
        compiler_params=_cparams(),
        name=name,
    )(x, g.reshape(1, d), mod_l, w)


def _wkv_kernel(r_ref, k_ref, v_ref, zs_ref, wdec_ref, wicl_ref, w0_ref, a0_ref, kk_ref, ka_ref, rk_ref, s0_ref,
                y_ref, bonus_ref, sfin_ref, state_ref, *, reverse, n_chunks, n_pp):
    jt = pl.program_id(2)
    c = CHUNK
    order = list(range(n_chunks - 1, -1, -1) if reverse else range(n_chunks))
    per_wave = n_chunks // WKV_WAVES
    waves = [[(pi, ci) for ci in order[w * per_wave:(w + 1) * per_wave] for pi in range(n_pp)]
             for w in range(WKV_WAVES)]

    @pl.when(jt == 0)
    def _():
        state_ref[...] = jnp.zeros(state_ref.shape, F32)
        for hi in range(2 * n_pp):
            lo = (hi % 2) * HEAD_DIM
            state_ref[hi // 2, lo:lo + HEAD_DIM, lo:lo + HEAD_DIM] = s0_ref[0, hi]

    lane = lax.broadcasted_iota(jnp.int32, (PAIR, PAIR), 1)
    row = lax.broadcasted_iota(jnp.int32, (PAIR, PAIR), 0)
    own = (lane // HEAD_DIM) == (row // HEAD_DIM)
    head0_row = lax.broadcasted_iota(jnp.int32, (1, PAIR), 1) < HEAD_DIM
    if reverse:
        strict, incl = lane > row, lane >= row
    else:
        strict, incl = row > lane, row >= lane
    head0 = lax.broadcasted_iota(jnp.int32, (c, PAIR), 1) < HEAD_DIM

    def head_sum(x):
        s0 = jnp.sum(jnp.where(head0_row, x, 0.0), axis=-1, keepdims=True)
        s1 = jnp.sum(jnp.where(head0_row, 0.0, x), axis=-1, keepdims=True)
        return jnp.where(head0_row, s0, s1)

    def swap_heads(x):
        return pltpu.roll(x, HEAD_DIM, 1)

    def stack(x):
        return jnp.concatenate([jnp.where(head0, x, 0.0), jnp.where(head0, 0.0, x)], axis=0)

    def chunk_operands(units, src):
        r, k_d, v, a_vec, b_vec, log_decay = src
        tri = incl[:c, :c].astype(F32)
        cut = lambda x, u: x[u[1] * c:(u[1] + 1) * c, u[0] * PAIR:(u[0] + 1) * PAIR]
        lw = [cut(log_decay, u) for u in units]
        cum = [_dot_split_rhs(tri, x) for x in lw]
        tot = [x[0:1] if reverse else x[c - 1:c] for x in cum]
        g_in = [jnp.exp(x) for x in cum]
        g_ex = [jnp.exp(x - y) for x, y in zip(cum, lw)]
        g_inv = [jnp.exp(-x) for x in cum]
        g_tot = [jnp.exp(t) for t in tot]
        g_hat = [t * x for t, x in zip(g_tot, g_inv)]
        aa = [stack(cut(a_vec, u) * g) for u, g in zip(units, g_ex)]
        ra = [stack(cut(r, u) * g) for u, g in zip(units, g_in)]
        bb = [stack((cut(b_vec, u) * g).astype(BF16)) for u, g in zip(units, g_inv)]
        kt = [stack((cut(k_d, u) * g).astype(BF16)) for u, g in zip(units, g_inv)]
        bh = [stack((cut(b_vec, u) * g).astype(BF16)) for u, g in zip(units, g_hat)]
        kh = [stack((cut(k_d, u) * g).astype(BF16)) for u, g in zip(units, g_hat)]
        vm = [stack(cut(v, u)) for u in units]
        vsw = [swap_heads(w) for w in vm]
        vm = [w.astype(BF16) for w in vm]
        return g_tot, aa, ra, bb, kt, bh, kh, vm, vsw

    def chunk_local(ops):
        g_tot, aa, ra, bb, kt, bh, kh, vm, vsw = ops
        scores = [_dot_nt(jnp.concatenate([a, b], axis=0), jnp.concatenate([e, f], axis=0))
                  for a, b, e, f in zip(aa, ra, bb, kt)]
        n_pow = [jnp.where(strict, s[:PAIR, :PAIR], 0.0) for s in scores]
        n_ak = [jnp.where(strict, s[:PAIR, PAIR:], 0.0) for s in scores]
        m_rb = [jnp.where(incl, s[PAIR:, :PAIR], 0.0) for s in scores]
        m_rk = [jnp.where(incl, s[PAIR:, PAIR:], 0.0) for s in scores]
        xs = [a + swap_heads(_dot(n, w)) for a, n, w in zip(aa, n_ak, vm)]
        steps = CHUNK.bit_length() - 1
        for si in range(steps):
            if si + 1 < steps:
                n_pow = [n.astype(BF16) for n in n_pow]
                prod = [_dot(n, jnp.concatenate([x.astype(BF16), n], axis=1)) for n, x in zip(n_pow, xs)]
                xs = [x + p[:, :PAIR] for x, p in zip(xs, prod)]
                n_pow = [p[:, PAIR:] for p in prod]
            else:
                xs = [x + _dot(n, x) for n, x in zip(n_pow, xs)]
        xv = [jnp.concatenate([x, w], axis=0) for x, w in zip(xs, vsw)]
        oc = [_dot(jnp.concatenate([mb, mk], axis=1), w) for mb, mk, w in zip(m_rb, m_rk, xv)]
        rh = [x + jnp.where(own, o, 0.0) for x, o in zip(ra, oc)]
        y_loc = [swap_heads(jnp.where(own, 0.0, o)) for o in oc]
        rq = [_dot_tn(w, jnp.concatenate([b, k_], axis=0)) for w, b, k_ in zip(xv, bh, kh)]
        pp = [jnp.where(own, q, 0.0) for q in rq]
        qq = [jnp.where(own, jnp.concatenate([q[HEAD_DIM:], q[:HEAD_DIM]], axis=0), 0.0) for q in rq]
        return g_tot, pp, qq, rh, y_loc

    zs = zs_ref[...].astype(F32)
    w_lin = w0_ref[0] + _dot(jnp.tanh(zs), wdec_ref[0])
    log_decay = -jnp.exp(-0.5) * _sigmoid(w_lin)
    a_lr = _sigmoid(a0_ref[0] + _dot(zs, wicl_ref[0]))
    r = r_ref[...].astype(F32)
    k = k_ref[...].astype(F32)
    v = v_ref[...].astype(F32)
    kk = k * kk_ref[...]
    k_d = k * (1.0 + (a_lr - 1.0) * ka_ref[...])
    kk_sq = kk * kk
    rkr = r * k_d * rk_ref[...]
    lanes = lambda x, pi: x[:, pi * PAIR:(pi + 1) * PAIR]
    kk_norm = jnp.concatenate([jnp.sqrt(head_sum(lanes(kk_sq, pi))) for pi in range(n_pp)], axis=1)
    bonus = jnp.concatenate([head_sum(lanes(rkr, pi)) for pi in range(n_pp)], axis=1) * v
    bonus_ref[0] = bonus.astype(bonus_ref.dtype)
    kk = kk / jnp.maximum(kk_norm, L2_EPS)
    src = (r, k_d, v, -kk, kk * a_lr, log_decay)

    local = {}
    for units in waves:
        for u, vals in zip(units, zip(*chunk_local(chunk_operands(units, src)))):
            local[u] = vals

    state = [state_ref[pi] for pi in range(n_pp)]
    ys = {}
    for ci in order:
        for pi in range(n_pp):
            g_tot, pp, qq, rh, y_loc = local[(pi, ci)]
            ym = y_loc + _dot_nt(rh, state[pi])
            state[pi] = state[pi] * g_tot + _dot(state[pi], pp) + qq
            ys[(pi, ci)] = ym[:c] + ym[c:]
    y_ref[0] = jnp.concatenate(
        [jnp.concatenate([ys[(pi, ci)] for pi in range(n_pp)], axis=1) for ci in range(n_chunks)],
        axis=0).astype(y_ref.dtype)
    for pi in range(n_pp):
        state_ref[pi] = state[pi]

    @pl.when(jt == pl.num_programs(2) - 1)
    def _():
        for hi in range(2 * n_pp):
            lo = (hi % 2) * HEAD_DIM
            sfin_ref[0, hi] = state[hi // 2][lo:lo + HEAD_DIM, lo:lo + HEAD_DIM]


def _wkv(z, s0p, wdec, wicl, layer, w0, a0, k_k, k_a, r_k, *, n_seq, seq_len, tb, reverse, col_r, col_k, col_v,
         col_s):
    t = z.shape[0]
    n_pairs = k_k.shape[1] // PAIR
    nb = seq_len // tb
    d = 1 if reverse else 0
    n_pp = WKV_PAIRS_PER_STEP
    wl = n_pp * PAIR
    assert (tb // CHUNK) % 2 == 0 and n_pairs % n_pp == 0 and seq_len % tb == 0

    def tblk(b, j):
        return b * nb + ((nb - 1 - j) if reverse else j)

    kern = functools.partial(_wkv_kernel, reverse=reverse, n_chunks=tb // CHUNK, n_pp=n_pp)
    tok = lambda col: pl.BlockSpec((tb, wl), lambda b, p, j: (tblk(b, j), col // wl + p))
    vec = pl.BlockSpec((1, wl), lambda b, p, j: (0, p))
    dvec = pl.BlockSpec((1, 1, wl), lambda b, p, j: (d, 0, p))
    dmat = pl.BlockSpec((None, 1, R_SMALL, wl), lambda b, p, j: (layer, d, 0, p))
    st = pl.BlockSpec((1, 2 * n_pp, HEAD_DIM, HEAD_DIM), lambda b, p, j: (b, p, 0, 0))
    out_tok = pl.BlockSpec((1, tb, wl), lambda b, p, j: (0, tblk(b, j), p))
    y, bonus, sfin = pl.pallas_call(
        kern,
        grid=(n_seq, n_pairs // n_pp, nb),
        in_specs=[tok(col_r), tok(col_k), tok(col_v),
                  pl.BlockSpec((tb, R_SMALL), lambda b, p, j: (tblk(b, j), col_s // R_SMALL)),
                  dmat, dmat, dvec, dvec, vec, vec, vec, st],
        out_specs=[out_tok, out_tok, st],
        out_shape=[jax.ShapeDtypeStruct((1, t, n_pairs * PAIR), BF16),
                   jax.ShapeDtypeStruct((1, t, n_pairs * PAIR), BF16),
                   jax.ShapeDtypeStruct((n_seq, 2 * n_pairs, HEAD_DIM, HEAD_DIM), F32)],
        scratch_shapes=[pltpu.VMEM((n_pp, PAIR, PAIR), F32)],
        compiler_params=_cparams(),
        name="wkv_bwd" if reverse else "wkv_fwd",
    )(z, z, z, z, wdec, wicl, w0, a0, k_k, k_a, r_k, s0p)
    return y[0], bonus[0], sfin


def _shift_prev(x, period):
    rolled = pltpu.roll(x, 1, 0)
    if period == x.shape[0]:
        t = lax.broadcasted_iota(jnp.int32, (SUBLANES, x.shape[1]), 0)
        return jnp.concatenate([jnp.where(t == 0, 0.0, rolled[:SUBLANES]), rolled[SUBLANES:]], axis=0)
    t = lax.broadcasted_iota(jnp.int32, x.shape, 0)
    return jnp.where((t & (period - 1)) == 0, 0.0, rolled)


def _shift_next(x, period):
    n = x.shape[0]
    rolled = pltpu.roll(x, n - 1, 0)
    if period == n:
        t = lax.broadcasted_iota(jnp.int32, (SUBLANES, x.shape[1]), 0)
        return jnp.concatenate([rolled[:n - SUBLANES], jnp.where(t == SUBLANES - 1, 0.0, rolled[n - SUBLANES:])],
                               axis=0)
    t = lax.broadcasted_iota(jnp.int32, x.shape, 0)
    return jnp.where((t & (period - 1)) == period - 1, 0.0, rolled)


def _mix_kernel(x_ref, yf_ref, yb_ref, bf_ref, bb_ref, gd_ref, cb_ref, cc_ref, cx_ref, ga_ref, gb_ref, mod_ref,
                gnw_ref, gnb_ref, g2_ref, cw_ref, cbias_ref, wpa_ref, wpb_ref, wo_ref, o_ref,
                *, tm, seq_len, row0, period):
    i = pl.program_id(0)
    d = x_ref.shape[1]
    lane = lax.broadcasted_iota(jnp.int32, (PAIR, PAIR), 1)
    row = lax.broadcasted_iota(jnp.int32, (PAIR, PAIR), 0)
    head_mean = jnp.where((lane // HEAD_DIM) == (row // HEAD_DIM), 1.0 / HEAD_DIM, 0.0).astype(F32)

    y = yf_ref[...].astype(F32) + yb_ref[...].astype(F32)
    parts = []
    for s in range(d // PAIR):
        ys = y[:, s * PAIR:(s + 1) * PAIR]
        mu = _dot(ys, head_mean)
        yc = ys - mu
        var = _dot(yc * yc, head_mean)
        parts.append(yc * lax.rsqrt(var + GN_EPS))
    yn = jnp.concatenate(parts, axis=1)
    o = yn * gnw_ref[...] + gnb_ref[...] + bf_ref[...].astype(F32) + bb_ref[...].astype(F32)
    gate = _dot(_sigmoid(gd_ref[...].astype(F32)), g2_ref[...])
    o_a = o * gate

    p = cc_ref[...].astype(F32) * cx_ref[...].astype(F32)
    cw = cw_ref[...]
    conv = _shift_prev(p, period) * cw[0:1] + p * cw[1:2] + _shift_next(p, period) * cw[2:3] + cbias_ref[...]
    o_b = cb_ref[...].astype(F32) * conv

    merged = (_sigmoid(ga_ref[...].astype(F32)) * _dot(o_a, wpa_ref[...])
              + _sigmoid(gb_ref[...].astype(F32)) * _dot(o_b, wpb_ref[...]))
    mrow = _mod_row(i, tm, seq_len, row0)
    gt1 = mod_ref[pl.ds(mrow, 1), 2 * d:3 * d]
    o_ref[...] = x_ref[...] + gt1 * _dot(merged, wo_ref[...])


def _mix(x, yf, yb, bonf, bonb, z, mod_l, gn_w, gn_b, g2, conv_w, conv_b, w_pa, w_pb, w_o, layer,
         *, tm, seq_len, row0, period, cols):
    t, d = x.shape
    stacked = lambda a: pl.BlockSpec((None,) + a.shape[1:], lambda i: (layer,) + (0,) * (a.ndim - 1))
    kern = functools.partial(_mix_kernel, tm=tm, seq_len=seq_len, row0=row0, period=period)
    tok = pl.BlockSpec((tm, d), lambda i: (i, 0))
    zcol = lambda col: pl.BlockSpec((tm, d), lambda i: (i, col // d))
    full = lambda a: pl.BlockSpec(a.shape, lambda i: (0,) * a.ndim)
    vec = pl.BlockSpec((1, d), lambda i: (0, 0))
    return pl.pallas_call(
        kern,
        grid=(t // tm,),
        in_specs=[tok, tok, tok, tok, tok,
                  pl.BlockSpec((tm, R_G), lambda i: (i, cols["gd"] // R_G)),
                  zcol(cols["cb"]), zcol(cols["cc"]), zcol(cols["cx"]), zcol(cols["ga"]), zcol(cols["gb"]),
                  full(mod_l), vec, vec, stacked(g2), full(conv_w), vec, stacked(w_pa), stacked(w_pb),
                  stacked(w_o)],
        out_specs=tok,
        out_shape=jax.ShapeDtypeStruct((t, d), F32),
        compiler_params=_cparams(),
        name="mix",
    )(x, yf, yb, bonf, bonb, z, z, z, z, z, z, mod_l, gn_w.reshape(1, d), gn_b.reshape(1, d), g2, conv_w,
      conv_b.reshape(1, d), w_pa, w_pb, w_o)


def _ffn_kernel(x_ref, u_ref, up_ref, un_ref, mod_ref, cw_ref, cb_ref, wd_ref, gf_ref, o_ref,
                *, tm, seq_len, row0, grid, final, kc):
    i = pl.program_id(0)
    d = x_ref.shape[1]
    d_ff = wd_ref.shape[0]
    tiles_per_seq = seq_len // tm
    first = (i % tiles_per_seq) == 0
    last = (i % tiles_per_seq) == tiles_per_seq - 1

    def conv(col):
        u = u_ref[:, col:col + kc].astype(F32)
        if grid:
            halo_p = jnp.where(first, 0.0, up_ref[:, col:col + kc].astype(F32))
            halo_n = jnp.where(last, 0.0, un_ref[:, col:col + kc].astype(F32))
            prev = jnp.concatenate([halo_p, u[:tm - GRID_W]], axis=0)
            nxt = jnp.concatenate([u[GRID_W:], halo_n], axis=0)
        else:
            prev = _shift_prev(u, seq_len)
            nxt = _shift_next(u, seq_len)
        cw = cw_ref[:, col:col + kc]
        return prev * cw[0:1] + u * cw[1:2] + nxt * cw[2:3] + cb_ref[:, col:col + kc]

    acc = jnp.zeros((tm, d), F32)
    for ci in range(d_ff // kc):
        act = conv(ci * kc)
        lin = conv(d_ff + ci * kc)
        acc = acc + _dot(_silu(act) * lin, wd_ref[ci * kc:(ci + 1) * kc, :])
    mrow = _mod_row(i, tm, seq_len, row0)
    gt2 = mod_ref[pl.ds(mrow, 1), 5 * d:6 * d]
    xn = x_ref[...] + gt2 * acc
    if final:
        xn = xn * lax.rsqrt(jnp.mean(xn * xn, axis=-1, keepdims=True) + NORM_EPS) * gf_ref[...]
    o_ref[...] = xn


def _ffn(x, u, mod_l, conv_w, conv_b, w_down, layer, norm_f_g, *, tm, seq_len, row0, grid, final):
    t, d = x.shape
    n_up = u.shape[1]
    hb = tm // GRID_W
    n_halo = t // GRID_W
    kern = functools.partial(_ffn_kernel, tm=tm, seq_len=seq_len, row0=row0, grid=grid, final=final,
                             kc=FFN_COL_CHUNK)
    full = lambda a: pl.BlockSpec(a.shape, lambda i: (0,) * a.ndim)
    return pl.pallas_call(
        kern,
        grid=(t // tm,),
        in_specs=[pl.BlockSpec((tm, d), lambda i: (i, 0)),
                  pl.BlockSpec((tm, n_up), lambda i: (i, 0)),
                  pl.BlockSpec((GRID_W, n_up), lambda i: (jnp.maximum(i * hb - 1, 0), 0)),
                  pl.BlockSpec((GRID_W, n_up), lambda i: (jnp.minimum(i * hb + hb, n_halo - 1), 0)),
                  full(mod_l), full(conv_w), pl.BlockSpec((1, n_up), lambda i: (0, 0)),
                  pl.BlockSpec((None,) + w_down.shape[1:], lambda i: (layer, 0, 0)),
                  pl.BlockSpec((1, d), lambda i: (0, 0))],
        out_specs=pl.BlockSpec((tm, d), lambda i: (i, 0)),
        out_shape=jax.ShapeDtypeStruct((t, d), F32),
        compiler_params=_cparams(),
        name="ffn",
    )(x, u, u, u, mod_l, conv_w, conv_b.reshape(1, n_up), w_down, norm_f_g.reshape(1, d))


def kernel(x_prompt, x_sample, state_wkv, c, c_ctx, w_mod, b_mod, norm1_g, w_in, decay_w0, decay_w2, iclr_a0,
           iclr_a2, gate_g2, k_k, k_a, r_k, gn_w, gn_b, conv_mix_w, conv_mix_b, w_pa, w_pb, w_o, norm2_g, w_up,
           conv_ffn_w, conv_ffn_b, w_down, norm_f_g):
    n_ctx, ctx_len, d = x_prompt.shape
    n_lat, lat_len, _ = x_sample.shape
    n_layers = w_mod.shape[0]
    d_a = k_k.shape[1]
    n_heads = d_a // HEAD_DIM
    r_w = decay_w2.shape[2]
    assert d_a == d and 4 * r_w == R_SMALL and gate_g2.shape[1] == R_G

    o_small = 3 * d_a
    o_gd = o_small + R_SMALL
    o_cb = o_gd + R_G
    o_gates = o_cb + 3 * d
    cols = dict(r=0, k=d, v=2 * d, cb=3 * d, cc=4 * d, cx=5 * d, ga=6 * d, gb=7 * d, small=8 * d,
                gd=8 * d + R_SMALL)
    n_in = 8 * d + R_SMALL + R_G
    n_in_pad = 8 * d + 512
    w_in_p = jnp.concatenate(
        [w_in[:, :, :o_small], w_in[:, :, o_cb:o_gates], w_in[:, :, o_gates:], w_in[:, :, o_small:o_cb],
         jnp.zeros((n_layers, d, n_in_pad - n_in), w_in.dtype)], axis=2).astype(BF16)
    w_up_b = w_up.astype(BF16)
    w_pa_b, w_pb_b, w_o_b = w_pa.astype(BF16), w_pb.astype(BF16), w_o.astype(BF16)
    w_down_b = w_down.astype(BF16)
    g2_b = gate_g2.astype(BF16)

    zrow = jnp.zeros((n_layers, r_w, d_a), F32)
    wdec = jnp.stack([jnp.concatenate([decay_w2[:, 0], zrow, zrow, zrow], axis=1),
                      jnp.concatenate([zrow, decay_w2[:, 1], zrow, zrow], axis=1)], axis=1)
    wicl = jnp.stack([jnp.concatenate([zrow, zrow, iclr_a2[:, 0], zrow], axis=1),
                      jnp.concatenate([zrow, zrow, zrow, iclr_a2[:, 1]], axis=1)], axis=1)

    cond8 = jnp.concatenate([c_ctx[None, :], c, jnp.zeros((8 - 1 - n_lat, d), F32)], axis=0)
    mod = _modulation(cond8, w_mod, b_mod)

    def layer(x, l, *, n_seq, seq_len, row0, grid, s0, final):
        tm = TOKEN_TILE
        z = _norm_mm(x, norm1_g[l], mod[l], w_in_p, l, tm=MM_TOKEN_TILE, tn=n_in_pad // 4, seq_len=seq_len,
                     row0=row0, sc_col=1, sh_col=0, name="in_proj")
        outs = []
        for rev in (False, True):
            dd = 1 if rev else 0
            outs.append(_wkv(z, s0[dd], wdec, wicl, l, decay_w0[l].reshape(2, 1, d_a),
                             iclr_a0[l].reshape(2, 1, d_a), k_k[l].reshape(1, d_a), k_a[l].reshape(1, d_a),
                             r_k[l].reshape(1, d_a), n_seq=n_seq, seq_len=seq_len, tb=WKV_BLOCK, reverse=rev,
                             col_r=cols["r"], col_k=cols["k"], col_v=cols["v"], col_s=cols["small"]))
        (yf, bonf, sf), (yb, bonb, sb) = outs
        x = _mix(x, yf, yb, bonf, bonb, z, mod[l], gn_w[l], gn_b[l], g2_b, conv_mix_w[l], conv_mix_b[l],
                 w_pa_b, w_pb_b, w_o_b, l, tm=tm, seq_len=seq_len, row0=row0,
                 period=GRID_W if grid else seq_len, cols=cols)
        u = _norm_mm(x, norm2_g[l], mod[l], w_up_b, l, tm=MM_TOKEN_TILE, tn=w_up.shape[2] // 4, seq_len=seq_len,
                     row0=row0, sc_col=4, sh_col=3, name="ffn_up")
        x = _ffn(x, u, mod[l], conv_ffn_w[l], conv_ffn_b[l], w_down_b, l, norm_f_g, tm=tm, seq_len=seq_len,
                 row0=row0, grid=grid, final=final)
        return x, sf, sb

    xp = x_prompt.reshape(n_ctx * ctx_len, d)
    zero_state = jnp.zeros((n_ctx, n_heads, HEAD_DIM, HEAD_DIM), F32)
    ctx_states = []
    for l in range(n_layers):
        xp, sf, sb = layer(xp, l, n_seq=n_ctx, seq_len=ctx_len, row0=0, grid=False,
                           s0=(zero_state, zero_state), final=(l == n_layers - 1))
        ctx_states.append(jnp.stack([sf, sb], axis=1))
    new_state = jnp.stack(ctx_states, axis=1).astype(x_prompt.dtype)
    y_prompt = xp.reshape(n_ctx, ctx_len, d)

    xs = x_sample.reshape(n_lat * lat_len, d)
    for l in range(n_layers):
        s0 = (state_wkv[:, l, 0].astype(F32), state_wkv[:, l, 1].astype(F32))
        xs, _, _ = layer(xs, l, n_seq=n_lat, seq_len=lat_len, row0=1, grid=True, s0=s0,
                         final=(l == n_layers - 1))
    y_sample = xs.reshape(n_lat, lat_len, d)
    return (y_prompt, y_sample, new_state)
```

```python
import functools

import jax
import jax.numpy as jnp
from jax import lax
from jax.experimental import pallas as pl
from jax.experimental.pallas import tpu as pltpu

F32 = jnp.float32
BF16 = jnp.bfloat16
HIGHEST = lax.Precision.HIGHEST

HEAD_DIM = 64
PAIR = 2 * HEAD_DIM
CHUNK = 64
WKV_PAIRS_PER_STEP = 8
WKV_WAVES = 2
WKV_BLOCK = 256
TOKEN_TILE = 512
MM_TOKEN_TILE = 1024
FFN_COL_CHUNK = 256
GRID_W = 64
R_SMALL = 256
R_G = 128
NORM_EPS = 1e-6
GN_EPS = 64e-5
L2_EPS = 1e-12
VMEM_LIMIT = 56 * 1024 * 1024


def _cparams():
    return pltpu.CompilerParams(vmem_limit_bytes=VMEM_LIMIT)


def _dot(a, b, hi=False):
    if hi:
        return jnp.dot(a, b, precision=HIGHEST, preferred_element_type=F32)
    return jnp.dot(a.astype(BF16), b.astype(BF16), preferred_element_type=F32)


def _dot_nt(a, b):
    return lax.dot_general(a.astype(BF16), b.astype(BF16), (((1,), (1,)), ((), ())),
                           preferred_element_type=F32)


def _dot_tn(a, b):
    return jnp.dot(a.T.astype(BF16), b.astype(BF16), preferred_element_type=F32)


def _split(x):
    hi = x.astype(BF16)
    return hi, (x - hi.astype(F32)).astype(BF16)


def _dot_split_rhs(a, b):
    hi, lo = _split(b)
    ab = a.astype(BF16)
    return jnp.dot(ab, hi, preferred_element_type=F32) + jnp.dot(ab, lo, preferred_element_type=F32)


def _sigmoid(x):
    return 1.0 / (1.0 + jnp.exp(-x))


def _silu(x):
    return x * _sigmoid(x)


def _mod_kernel(c_ref, w_ref, b_ref, o_ref):
    s = _silu(c_ref[...])
    o_ref[0] = _dot(s, w_ref[0], hi=True) + b_ref[0]


def _modulation(cond8, w_mod, b_mod):
    n_layers, d, n = w_mod.shape
    tn = n // 4
    return pl.pallas_call(
        _mod_kernel,
        grid=(n_layers, n // tn),
        in_specs=[pl.BlockSpec((8, d), lambda l, j: (0, 0)),
                  pl.BlockSpec((1, d, tn), lambda l, j: (l, 0, j)),
                  pl.BlockSpec((1, 1, tn), lambda l, j: (l, 0, j))],
        out_specs=pl.BlockSpec((1, 8, tn), lambda l, j: (l, 0, j)),
        out_shape=jax.ShapeDtypeStruct((n_layers, 8, n), F32),
        compiler_params=_cparams(),
        name="modulation",
    )(cond8, w_mod, b_mod.reshape(n_layers, 1, n))


def _mod_row(i, tm, seq_len, row0):
    if row0 == 0:
        return 0
    return row0 + (i * tm) // seq_len


def _norm_mm_kernel(x_ref, g_ref, mod_ref, w_ref, o_ref, h_ref, *, tm, seq_len, row0, sc_col, sh_col):
    i = pl.program_id(0)
    d = x_ref.shape[1]

    @pl.when(pl.program_id(1) == 0)
    def _():
        x = x_ref[...]
        y = x * lax.rsqrt(jnp.mean(x * x, axis=-1, keepdims=True) + NORM_EPS) * g_ref[...]
        row = _mod_row(i, tm, seq_len, row0)
        sc = mod_ref[pl.ds(row, 1), sc_col * d:(sc_col + 1) * d]
        sh = mod_ref[pl.ds(row, 1), sh_col * d:(sh_col + 1) * d]
        h_ref[...] = (y * (1.0 + sc) + sh).astype(BF16)

    o_ref[...] = jnp.dot(h_ref[...], w_ref[...], preferred_element_type=F32).astype(o_ref.dtype)


def _norm_mm(x, g, mod_l, w, layer, *, tm, tn, seq_len, row0, sc_col, sh_col, name):
    t, d = x.shape
    n = w.shape[2]
    assert t % tm == 0 and n % tn == 0 and (row0 == 0 or seq_len % tm == 0)
    kern = functools.partial(_norm_mm_kernel, tm=tm, seq_len=seq_len, row0=row0, sc_col=sc_col, sh_col=sh_col)
    return pl.pallas_call(
        kern,
        grid=(t // tm, n // tn),
        in_specs=[pl.BlockSpec((tm, d), lambda i, j: (i, 0)),
                  pl.BlockSpec((1, d), lambda i, j: (0, 0)),
                  pl.BlockSpec(mod_l.shape, lambda i, j: (0, 0)),
                  pl.BlockSpec((None, d, tn), lambda i, j: (layer, 0, j))],
        out_specs=pl.BlockSpec((tm, tn), lambda i, j: (i, j)),
        out_shape=jax.ShapeDtypeStruct((t, n), BF16),
        scratch_shapes=[pltpu.VMEM((tm, d), BF16)],
        compiler_params=_cparams(),
        name=name,
    )(x, g.reshape(1, d), mod_l, w)


def _wkv_kernel(r_ref, k_ref, v_ref, zs_ref, wdec_ref, wicl_ref, w0_ref, a0_ref, kk_ref, ka_ref, rk_ref, s0_ref,
                y_ref, bonus_ref, sfin_ref, state_ref, *, reverse, n_chunks, n_pp):
    jt = pl.program_id(2)
    c = CHUNK
    order = list(range(n_chunks - 1, -1, -1) if reverse else range(n_chunks))
    per_wave = n_chunks // WKV_WAVES
    waves = [[(pi, ci) for ci in order[w * per_wave:(w + 1) * per_wave] for pi in range(n_pp)]
             for w in range(WKV_WAVES)]

    @pl.when(jt == 0)
    def _():
        state_ref[...] = jnp.zeros(state_ref.shape, F32)
        for hi in range(2 * n_pp):
            lo = (hi % 2) * HEAD_DIM
            state_ref[hi // 2, lo:lo + HEAD_DIM, lo:lo + HEAD_DIM] = s0_ref[0, hi]

    lane = lax.broadcasted_iota(jnp.int32, (PAIR, PAIR), 1)
    row = lax.broadcasted_iota(jnp.int32, (PAIR, PAIR), 0)
    own = (lane // HEAD_DIM) == (row // HEAD_DIM)
    head0_row = lax.broadcasted_iota(jnp.int32, (1, PAIR), 1) < HEAD_DIM
    if reverse:
        strict, incl = lane > row, lane >= row
    else:
        strict, incl = row > lane, row >= lane
    head0 = lax.broadcasted_iota(jnp.int32, (c, PAIR), 1) < HEAD_DIM

    def head_sum(x):
        s0 = jnp.sum(jnp.where(head0_row, x, 0.0), axis=-1, keepdims=True)
        s1 = jnp.sum(jnp.where(head0_row, 0.0, x), axis=-1, keepdims=True)
        return jnp.where(head0_row, s0, s1)

    def swap_heads(x):
        return pltpu.roll(x, HEAD_DIM, 1)

    def stack(x):
        return jnp.concatenate([jnp.where(head0, x, 0.0), jnp.where(head0, 0.0, x)], axis=0)

    def chunk_operands(units, src):
        r, k_d, v, a_vec, b_vec, log_decay = src
        tri = incl[:c, :c].astype(F32)
        cut = lambda x, u: x[u[1] * c:(u[1] + 1) * c, u[0] * PAIR:(u[0] + 1) * PAIR]
        lw = [cut(log_decay, u) for u in units]
        cum = [_dot_split_rhs(tri, x) for x in lw]
        tot = [x[0:1] if reverse else x[c - 1:c] for x in cum]
        g_in = [jnp.exp(x) for x in cum]
        g_ex = [jnp.exp(x - y) for x, y in zip(cum, lw)]
        g_inv = [jnp.exp(-x) for x in cum]
        g_tot = [jnp.exp(t) for t in tot]
        g_hat = [t * x for t, x in zip(g_tot, g_inv)]
        aa = [stack(cut(a_vec, u) * g) for u, g in zip(units, g_ex)]
        ra = [stack(cut(r, u) * g) for u, g in zip(units, g_in)]
        bb = [stack((cut(b_vec, u) * g).astype(BF16)) for u, g in zip(units, g_inv)]
        kt = [stack((cut(k_d, u) * g).astype(BF16)) for u, g in zip(units, g_inv)]
        bh = [stack((cut(b_vec, u) * g).astype(BF16)) for u, g in zip(units, g_hat)]
        kh = [stack((cut(k_d, u) * g).astype(BF16)) for u, g in zip(units, g_hat)]
        vm = [stack(cut(v, u)) for u in units]
        vsw = [swap_heads(w) for w in vm]
        vm = [w.astype(BF16) for w in vm]
        return g_tot, aa, ra, bb, kt, bh, kh, vm, vsw

    def chunk_local(ops):
        g_tot, aa, ra, bb, kt, bh, kh, vm, vsw = ops
        scores = [_dot_nt(jnp.concatenate([a, b], axis=0), jnp.concatenate([e, f], axis=0))
                  for a, b, e, f in zip(aa, ra, bb, kt)]
        n_pow = [jnp.where(strict, s[:PAIR, :PAIR], 0.0) for s in scores]
        n_ak = [jnp.where(strict, s[:PAIR, PAIR:], 0.0) for s in scores]
        m_rb = [jnp.where(incl, s[PAIR:, :PAIR], 0.0) for s in scores]
        m_rk = [jnp.where(incl, s[PAIR:, PAIR:], 0.0) for s in scores]
        xs = [a + swap_heads(_dot(n, w)) for a, n, w in zip(aa, n_ak, vm)]
        steps = CHUNK.bit_length() - 1
        for si in range(steps):
            if si + 1 < steps:
                n_pow = [n.astype(BF16) for n in n_pow]
                prod = [_dot(n, jnp.concatenate([x.astype(BF16), n], axis=1)) for n, x in zip(n_pow, xs)]
                xs = [x + p[:, :PAIR] for x, p in zip(xs, prod)]
                n_pow = [p[:, PAIR:] for p in prod]
            else:
                xs = [x + _dot(n, x) for n, x in zip(n_pow, xs)]
        xv = [jnp.concatenate([x, w], axis=0) for x, w in zip(xs, vsw)]
        oc = [_dot(jnp.concatenate([mb, mk], axis=1), w) for mb, mk, w in zip(m_rb, m_rk, xv)]
        rh = [x + jnp.where(own, o, 0.0) for x, o in zip(ra, oc)]
        y_loc = [swap_heads(jnp.where(own, 0.0, o)) for o in oc]
        rq = [_dot_tn(w, jnp.concatenate([b, k_], axis=0)) for w, b, k_ in zip(xv, bh, kh)]
        pp = [jnp.where(own, q, 0.0) for q in rq]
        qq = [jnp.where(own, jnp.concatenate([q[HEAD_DIM:], q[:HEAD_DIM]], axis=0), 0.0) for q in rq]
        return g_tot, pp, qq, rh, y_loc

    zs = zs_ref[...].astype(F32)
    w_lin = w0_ref[0] + _dot(jnp.tanh(zs), wdec_ref[0])
    log_decay = -jnp.exp(-0.5) * _sigmoid(w_lin)
    a_lr = _sigmoid(a0_ref[0] + _dot(zs, wicl_ref[0]))
    r = r_ref[...].astype(F32)
    k = k_ref[...].astype(F32)
    v = v_ref[...].astype(F32)
    kk = k * kk_ref[...]
    k_d = k * (1.0 + (a_lr - 1.0) * ka_ref[...])
    kk_sq = kk * kk
    rkr = r * k_d * rk_ref[...]
    lanes = lambda x, pi: x[:, pi * PAIR:(pi + 1) * PAIR]
    kk_norm = jnp.concatenate([jnp.sqrt(head_sum(lanes(kk_sq, pi))) for pi in range(n_pp)], axis=1)
    bonus = jnp.concatenate([head_sum(lanes(rkr, pi)) for pi in range(n_pp)], axis=1) * v
    bonus_ref[0] = bonus.astype(bonus_ref.dtype)
    kk = kk / jnp.maximum(kk_norm, L2_EPS)
    src = (r, k_d, v, -kk, kk * a_lr, log_decay)

    local = {}
    for units in waves:
        for u, vals in zip(units, zip(*chunk_local(chunk_operands(units, src)))):
            local[u] = vals

    state = [state_ref[pi] for pi in range(n_pp)]
    ys = {}
    for ci in order:
        for pi in range(n_pp):
            g_tot, pp, qq, rh, y_loc = local[(pi, ci)]
            ym = y_loc + _dot_nt(rh, state[pi])
            state[pi] = state[pi] * g_tot + _dot(state[pi], pp) + qq
            ys[(pi, ci)] = ym[:c] + ym[c:]
    y_ref[0] = jnp.concatenate(
        [jnp.concatenate([ys[(pi, ci)] for pi in range(n_pp)], axis=1) for ci in range(n_chunks)],
        axis=0).astype(y_ref.dtype)
    for pi in range(n_pp):
        state_ref[pi] = state[pi]

    @pl.when(jt == pl.num_programs(2) - 1)
    def _():
        for hi in range(2 * n_pp):
            lo = (hi % 2) * HEAD_DIM
            sfin_ref[0, hi] = state[hi // 2][lo:lo + HEAD_DIM, lo:lo + HEAD_DIM]


def _wkv(z, s0p, wdec, wicl, layer, w0, a0, k_k, k_a, r_k, *, n_seq, seq_len, tb, reverse, col_r, col_k, col_v,
         col_s):
    t = z.shape[0]
    n_pairs = k_k.shape[1] // PAIR
    nb = seq_len // tb
    d = 1 if reverse else 0
    n_pp = WKV_PAIRS_PER_STEP
    wl = n_pp * PAIR
    assert (tb // CHUNK) % 2 == 0 and n_pairs % n_pp == 0 and seq_len % tb == 0

    def tblk(b, j):
        return b * nb + ((nb - 1 - j) if reverse else j)

    kern = functools.partial(_wkv_kernel, reverse=reverse, n_chunks=tb // CHUNK, n_pp=n_pp)
    tok = lambda col: pl.BlockSpec((tb, wl), lambda b, p, j: (tblk(b, j), col // wl + p))
    vec = pl.BlockSpec((1, wl), lambda b, p, j: (0, p))
    dvec = pl.BlockSpec((1, 1, wl), lambda b, p, j: (d, 0, p))
    dmat = pl.BlockSpec((None, 1, R_SMALL, wl), lambda b, p, j: (layer, d, 0, p))
    st = pl.BlockSpec((1, 2 * n_pp, HEAD_DIM, HEAD_DIM), lambda b, p, j: (b, p, 0, 0))
    out_tok = pl.BlockSpec((1, tb, wl), lambda b, p, j: (0, tblk(b, j), p))
    y, bonus, sfin = pl.pallas_call(
        kern,
        grid=(n_seq, n_pairs // n_pp, nb),
        in_specs=[tok(col_r), tok(col_k), tok(col_v),
                  pl.BlockSpec((tb, R_SMALL), lambda b, p, j: (tblk(b, j), col_s // R_SMALL)),
                  dmat, dmat, dvec, dvec, vec, vec, vec, st],
        out_specs=[out_tok, out_tok, st],
        out_shape=[jax.ShapeDtypeStruct((1, t, n_pairs * PAIR), BF16),
                   jax.ShapeDtypeStruct((1, t, n_pairs * PAIR), BF16),
                   jax.ShapeDtypeStruct((n_seq, 2 * n_pairs, HEAD_DIM, HEAD_DIM), F32)],
        scratch_shapes=[pltpu.VMEM((n_pp, PAIR, PAIR), F32)],
        compiler_params=_cparams(),
        name="wkv_bwd" if reverse else "wkv_fwd",
    )(z, z, z, z, wdec, wicl, w0, a0, k_k, k_a, r_k, s0p)
    return y[0], bonus[0], sfin


def _shift_prev(x, period):
    t = lax.broadcasted_iota(jnp.int32, x.shape, 0)
    return jnp.where((t & (period - 1)) == 0, 0.0, pltpu.roll(x, 1, 0))


def _shift_next(x, period):
    t = lax.broadcasted_iota(jnp.int32, x.shape, 0)
    return jnp.where((t & (period - 1)) == period - 1, 0.0, pltpu.roll(x, x.shape[0] - 1, 0))


def _mix_kernel(x_ref, yf_ref, yb_ref, bf_ref, bb_ref, gd_ref, cb_ref, cc_ref, cx_ref, ga_ref, gb_ref, mod_ref,
                gnw_ref, gnb_ref, g2_ref, cw_ref, cbias_ref, wpa_ref, wpb_ref, wo_ref, o_ref,
                *, tm, seq_len, row0, period):
    i = pl.program_id(0)
    d = x_ref.shape[1]
    lane = lax.broadcasted_iota(jnp.int32, (PAIR, PAIR), 1)
    row = lax.broadcasted_iota(jnp.int32, (PAIR, PAIR), 0)
    head_mean = jnp.where((lane // HEAD_DIM) == (row // HEAD_DIM), 1.0 / HEAD_DIM, 0.0).astype(F32)

    y = yf_ref[...].astype(F32) + yb_ref[...].astype(F32)
    parts = []
    for s in range(d // PAIR):
        ys = y[:, s * PAIR:(s + 1) * PAIR]
        mu = _dot(ys, head_mean)
        yc = ys - mu
        var = _dot(yc * yc, head_mean)
        parts.append(yc * lax.rsqrt(var + GN_EPS))
    yn = jnp.concatenate(parts, axis=1)
    o = yn * gnw_ref[...] + gnb_ref[...] + bf_ref[...].astype(F32) + bb_ref[...].astype(F32)
    gate = _dot(_sigmoid(gd_ref[...].astype(F32)), g2_ref[...])
    o_a = o * gate

    p = cc_ref[...].astype(F32) * cx_ref[...].astype(F32)
    cw = cw_ref[...]
    conv = _shift_prev(p, period) * cw[0:1] + p * cw[1:2] + _shift_next(p, period) * cw[2:3] + cbias_ref[...]
    o_b = cb_ref[...].astype(F32) * conv

    merged = (_sigmoid(ga_ref[...].astype(F32)) * _dot(o_a, wpa_ref[...])
              + _sigmoid(gb_ref[...].astype(F32)) * _dot(o_b, wpb_ref[...]))
    mrow = _mod_row(i, tm, seq_len, row0)
    gt1 = mod_ref[pl.ds(mrow, 1), 2 * d:3 * d]
    o_ref[...] = x_ref[...] + gt1 * _dot(merged, wo_ref[...])


def _mix(x, yf, yb, bonf, bonb, z, mod_l, gn_w, gn_b, g2, conv_w, conv_b, w_pa, w_pb, w_o, layer,
         *, tm, seq_len, row0, period, cols):
    t, d = x.shape
    stacked = lambda a: pl.BlockSpec((None,) + a.shape[1:], lambda i: (layer,) + (0,) * (a.ndim - 1))
    kern = functools.partial(_mix_kernel, tm=tm, seq_len=seq_len, row0=row0, period=period)
    tok = pl.BlockSpec((tm, d), lambda i: (i, 0))
    zcol = lambda col: pl.BlockSpec((tm, d), lambda i: (i, col // d))
    full = lambda a: pl.BlockSpec(a.shape, lambda i: (0,) * a.ndim)
    vec = pl.BlockSpec((1, d), lambda i: (0, 0))
    return pl.pallas_call(
        kern,
        grid=(t // tm,),
        in_specs=[tok, tok, tok, tok, tok,
                  pl.BlockSpec((tm, R_G), lambda i: (i, cols["gd"] // R_G)),
                  zcol(cols["cb"]), zcol(cols["cc"]), zcol(cols["cx"]), zcol(cols["ga"]), zcol(cols["gb"]),
                  full(mod_l), vec, vec, stacked(g2), full(conv_w), vec, stacked(w_pa), stacked(w_pb),
                  stacked(w_o)],
        out_specs=tok,
        out_shape=jax.ShapeDtypeStruct((t, d), F32),
        compiler_params=_cparams(),
        name="mix",
    )(x, yf, yb, bonf, bonb, z, z, z, z, z, z, mod_l, gn_w.reshape(1, d), gn_b.reshape(1, d), g2, conv_w,
      conv_b.reshape(1, d), w_pa, w_pb, w_o)


def _ffn_kernel(x_ref, u_ref, up_ref, un_ref, mod_ref, cw_ref, cb_ref, wd_ref, gf_ref, o_ref,
                *, tm, seq_len, row0, grid, final, kc):
    i = pl.program_id(0)
    d = x_ref.shape[1]
    d_ff = wd_ref.shape[0]
    tiles_per_seq = seq_len // tm
    first = (i % tiles_per_seq) == 0
    last = (i % tiles_per_seq) == tiles_per_seq - 1

    def conv(col):
        u = u_ref[:, col:col + kc].astype(F32)
        if grid:
            halo_p = jnp.where(first, 0.0, up_ref[:, col:col + kc].astype(F32))
            halo_n = jnp.where(last, 0.0, un_ref[:, col:col + kc].astype(F32))
            prev = jnp.concatenate([halo_p, u[:tm - GRID_W]], axis=0)
            nxt = jnp.concatenate([u[GRID_W:], halo_n], axis=0)
        else:
            prev = _shift_prev(u, seq_len)
            nxt = _shift_next(u, seq_len)
        cw = cw_ref[:, col:col + kc]
        return prev * cw[0:1] + u * cw[1:2] + nxt * cw[2:3] + cb_ref[:, col:col + kc]

    acc = jnp.zeros((tm, d), F32)
    for ci in range(d_ff // kc):
        act = conv(ci * kc)
        lin = conv(d_ff + ci * kc)
        acc = acc + _dot(_silu(act.astype(BF16)) * lin.astype(BF16), wd_ref[ci * kc:(ci + 1) * kc, :])
    mrow = _mod_row(i, tm, seq_len, row0)
    gt2 = mod_ref[pl.ds(mrow, 1), 5 * d:6 * d]
    xn = x_ref[...] + gt2 * acc
    if final:
        xn = xn * lax.rsqrt(jnp.mean(xn * xn, axis=-1, keepdims=True) + NORM_EPS) * gf_ref[...]
    o_ref[...] = xn


def _ffn(x, u, mod_l, conv_w, conv_b, w_down, layer, norm_f_g, *, tm, seq_len, row0, grid, final):
    t, d = x.shape
    n_up = u.shape[1]
    hb = tm // GRID_W
    n_halo = t // GRID_W
    kern = functools.partial(_ffn_kernel, tm=tm, seq_len=seq_len, row0=row0, grid=grid, final=final,
                             kc=FFN_COL_CHUNK)
    full = lambda a: pl.BlockSpec(a.shape, lambda i: (0,) * a.ndim)
    return pl.pallas_call(
        kern,
        grid=(t // tm,),
        in_specs=[pl.BlockSpec((tm, d), lambda i: (i, 0)),
                  pl.BlockSpec((tm, n_up), lambda i: (i, 0)),
                  pl.BlockSpec((GRID_W, n_up), lambda i: (jnp.maximum(i * hb - 1, 0), 0)),
                  pl.BlockSpec((GRID_W, n_up), lambda i: (jnp.minimum(i * hb + hb, n_halo - 1), 0)),
                  full(mod_l), full(conv_w), pl.BlockSpec((1, n_up), lambda i: (0, 0)),
                  pl.BlockSpec((None,) + w_down.shape[1:], lambda i: (layer, 0, 0)),
                  pl.BlockSpec((1, d), lambda i: (0, 0))],
        out_specs=pl.BlockSpec((tm, d), lambda i: (i, 0)),
        out_shape=jax.ShapeDtypeStruct((t, d), F32),
        compiler_params=_cparams(),
        name="ffn",
    )(x, u, u, u, mod_l, conv_w, conv_b.reshape(1, n_up), w_down, norm_f_g.reshape(1, d))


def kernel(x_prompt, x_sample, state_wkv, c, c_ctx, w_mod, b_mod, norm1_g, w_in, decay_w0, decay_w2, iclr_a0,
           iclr_a2, gate_g2, k_k, k_a, r_k, gn_w, gn_b, conv_mix_w, conv_mix_b, w_pa, w_pb, w_o, norm2_g, w_up,
           conv_ffn_w, conv_ffn_b, w_down, norm_f_g):
    n_ctx, ctx_len, d = x_prompt.shape
    n_lat, lat_len, _ = x_sample.shape
    n_layers = w_mod.shape[0]
    d_a = k_k.shape[1]
    n_heads = d_a // HEAD_DIM
    r_w = decay_w2.shape[2]
    assert d_a == d and 4 * r_w == R_SMALL and gate_g2.shape[1] == R_G

    o_small = 3 * d_a
    o_gd = o_small + R_SMALL
    o_cb = o_gd + R_G
    o_gates = o_cb + 3 * d
    cols = dict(r=0, k=d, v=2 * d, cb=3 * d, cc=4 * d, cx=5 * d, ga=6 * d, gb=7 * d, small=8 * d,
                gd=8 * d + R_SMALL)
    n_in = 8 * d + R_SMALL + R_G
    n_in_pad = 8 * d + 512
    w_in_p = jnp.concatenate(
        [w_in[:, :, :o_small], w_in[:, :, o_cb:o_gates], w_in[:, :, o_gates:], w_in[:, :, o_small:o_cb],
         jnp.zeros((n_layers, d, n_in_pad - n_in), w_in.dtype)], axis=2).astype(BF16)
    w_up_b = w_up.astype(BF16)
    w_pa_b, w_pb_b, w_o_b = w_pa.astype(BF16), w_pb.astype(BF16), w_o.astype(BF16)
    w_down_b = w_down.astype(BF16)
    g2_b = gate_g2.astype(BF16)

    zrow = jnp.zeros((n_layers, r_w, d_a), F32)
    wdec = jnp.stack([jnp.concatenate([decay_w2[:, 0], zrow, zrow, zrow], axis=1),
                      jnp.concatenate([zrow, decay_w2[:, 1], zrow, zrow], axis=1)], axis=1)
    wicl = jnp.stack([jnp.concatenate([zrow, zrow, iclr_a2[:, 0], zrow], axis=1),
                      jnp.concatenate([zrow, zrow, zrow, iclr_a2[:, 1]], axis=1)], axis=1)

    cond8 = jnp.concatenate([c_ctx[None, :], c, jnp.zeros((8 - 1 - n_lat, d), F32)], axis=0)
    mod = _modulation(cond8, w_mod, b_mod)

    def layer(x, l, *, n_seq, seq_len, row0, grid, s0, final):
        tm = TOKEN_TILE
        z = _norm_mm(x, norm1_g[l], mod[l], w_in_p, l, tm=MM_TOKEN_TILE, tn=n_in_pad // 4, seq_len=seq_len,
                     row0=row0, sc_col=1, sh_col=0, name="in_proj")
        outs = []
        for rev in (False, True):
            dd = 1 if rev else 0
            outs.append(_wkv(z, s0[dd], wdec, wicl, l, decay_w0[l].reshape(2, 1, d_a),
                             iclr_a0[l].reshape(2, 1, d_a), k_k[l].reshape(1, d_a), k_a[l].reshape(1, d_a),
                             r_k[l].reshape(1, d_a), n_seq=n_seq, seq_len=seq_len, tb=WKV_BLOCK, reverse=rev,
                             col_r=cols["r"], col_k=cols["k"], col_v=cols["v"], col_s=cols["small"]))
        (yf, bonf, sf), (yb, bonb, sb) = outs
        x = _mix(x, yf, yb, bonf, bonb, z, mod[l], gn_w[l], gn_b[l], g2_b, conv_mix_w[l], conv_mix_b[l],
                 w_pa_b, w_pb_b, w_o_b, l, tm=tm, seq_len=seq_len, row0=row0,
                 period=GRID_W if grid else seq_len, cols=cols)
        u = _norm_mm(x, norm2_g[l], mod[l], w_up_b, l, tm=MM_TOKEN_TILE, tn=w_up.shape[2] // 4, seq_len=seq_len,
                     row0=row0, sc_col=4, sh_col=3, name="ffn_up")
        x = _ffn(x, u, mod[l], conv_ffn_w[l], conv_ffn_b[l], w_down_b, l, norm_f_g, tm=tm, seq_len=seq_len,
                 row0=row0, grid=grid, final=final)
        return x, sf, sb

    xp = x_prompt.reshape(n_ctx * ctx_len, d)
    zero_state = jnp.zeros((n_ctx, n_heads, HEAD_DIM, HEAD_DIM), F32)
    ctx_states = []
    for l in range(n_layers):
        xp, sf, sb = layer(xp, l, n_seq=n_ctx, seq_len=ctx_len, row0=0, grid=False,
                           s0=(zero_state, zero_state), final=(l == n_layers - 1))
        ctx_states.append(jnp.stack([sf, sb], axis=1))
    new_state = jnp.stack(ctx_states, axis=1).astype(x_prompt.dtype)
    y_prompt = xp.reshape(n_ctx, ctx_len, d)

    xs = x_sample.reshape(n_lat * lat_len, d)
    for l in range(n_layers):
        s0 = (state_wkv[:, l, 0].astype(F32), state_wkv[:, l, 1].astype(F32))
        xs, _, _ = layer(xs, l, n_seq=n_lat, seq_len=lat_len, row0=1, grid=True, s0=s0,
                         final=(l == n_layers - 1))
    y_sample = xs.reshape(n_lat, lat_len, d)
    return (y_prompt, y_sample, new_state)
```

```python
import functools

import jax
import jax.numpy as jnp
from jax import lax
from jax.experimental import pallas as pl
from jax.experimental.pallas import tpu as pltpu

F32 = jnp.float32
BF16 = jnp.bfloat16
HIGHEST = lax.Precision.HIGHEST

HEAD_DIM = 64
PAIR = 2 * HEAD_DIM
CHUNK = 64
WKV_PAIRS_PER_STEP = 8
WKV_WAVES = 2
WKV_BLOCK = 256
TOKEN_TILE = 512
MM_TOKEN_TILE = 1024
MM_COL_TILES = 2
FFN_COL_CHUNK = 256
GRID_W = 64
R_SMALL = 256
R_G = 128
NORM_EPS = 1e-6
GN_EPS = 64e-5
L2_EPS = 1e-12
VMEM_LIMIT = 56 * 1024 * 1024


def _cparams():
    return pltpu.CompilerParams(vmem_limit_bytes=VMEM_LIMIT)


def _dot(a, b, hi=False):
    if hi:
        return jnp.dot(a, b, precision=HIGHEST, preferred_element_type=F32)
    return jnp.dot(a.astype(BF16), b.astype(BF16), preferred_element_type=F32)


def _dot_nt(a, b):
    return lax.dot_general(a.astype(BF16), b.astype(BF16), (((1,), (1,)), ((), ())),
                           preferred_element_type=F32)


def _dot_tn(a, b):
    return jnp.dot(a.T.astype(BF16), b.astype(BF16), preferred_element_type=F32)


def _split(x):
    hi = x.astype(BF16)
    return hi, (x - hi.astype(F32)).astype(BF16)


def _dot_split_rhs(a, b):
    hi, lo = _split(b)
    ab = a.astype(BF16)
    return jnp.dot(ab, hi, preferred_element_type=F32) + jnp.dot(ab, lo, preferred_element_type=F32)


def _sigmoid(x):
    return 1.0 / (1.0 + jnp.exp(-x))


def _silu(x):
    return x * _sigmoid(x)


def _mod_kernel(c_ref, w_ref, b_ref, o_ref):
    s = _silu(c_ref[...])
    o_ref[0] = _dot(s, w_ref[0], hi=True) + b_ref[0]


def _modulation(cond8, w_mod, b_mod):
    n_layers, d, n = w_mod.shape
    tn = n // 4
    return pl.pallas_call(
        _mod_kernel,
        grid=(n_layers, n // tn),
        in_specs=[pl.BlockSpec((8, d), lambda l, j: (0, 0)),
                  pl.BlockSpec((1, d, tn), lambda l, j: (l, 0, j)),
                  pl.BlockSpec((1, 1, tn), lambda l, j: (l, 0, j))],
        out_specs=pl.BlockSpec((1, 8, tn), lambda l, j: (l, 0, j)),
        out_shape=jax.ShapeDtypeStruct((n_layers, 8, n), F32),
        compiler_params=_cparams(),
        name="modulation",
    )(cond8, w_mod, b_mod.reshape(n_layers, 1, n))


def _mod_row(i, tm, seq_len, row0):
    if row0 == 0:
        return 0
    return row0 + (i * tm) // seq_len


def _norm_mm_kernel(x_ref, g_ref, mod_ref, w_ref, o_ref, h_ref, *, tm, seq_len, row0, sc_col, sh_col):
    i = pl.program_id(0)
    d = x_ref.shape[1]

    @pl.when(pl.program_id(1) == 0)
    def _():
        x = x_ref[...]
        y = x * lax.rsqrt(jnp.mean(x * x, axis=-1, keepdims=True) + NORM_EPS) * g_ref[...]
        row = _mod_row(i, tm, seq_len, row0)
        sc = mod_ref[pl.ds(row, 1), sc_col * d:(sc_col + 1) * d]
        sh = mod_ref[pl.ds(row, 1), sh_col * d:(sh_col + 1) * d]
        h_ref[...] = (y * (1.0 + sc) + sh).astype(BF16)

    o_ref[...] = jnp.dot(h_ref[...], w_ref[...], preferred_element_type=F32).astype(o_ref.dtype)


def _norm_mm(x, g, mod_l, w, layer, *, tm, tn, seq_len, row0, sc_col, sh_col, name):
    t, d = x.shape
    n = w.shape[2]
    assert t % tm == 0 and n % tn == 0 and (row0 == 0 or seq_len % tm == 0)
    kern = functools.partial(_norm_mm_kernel, tm=tm, seq_len=seq_len, row0=row0, sc_col=sc_col, sh_col=sh_col)
    return pl.pallas_call(
        kern,
        grid=(t // tm, n // tn),
        in_specs=[pl.BlockSpec((tm, d), lambda i, j: (i, 0)),
                  pl.BlockSpec((1, d), lambda i, j: (0, 0)),
                  pl.BlockSpec(mod_l.shape, lambda i, j: (0, 0)),
                  pl.BlockSpec((None, d, tn), lambda i, j: (layer, 0, j))],
        out_specs=pl.BlockSpec((tm, tn), lambda i, j: (i, j)),
        out_shape=jax.ShapeDtypeStruct((t, n), BF16),
        scratch_shapes=[pltpu.VMEM((tm, d), BF16)],
        compiler_params=_cparams(),
        name=name,
    )(x, g.reshape(1, d), mod_l, w)


def _wkv_kernel(r_ref, k_ref, v_ref, zs_ref, wdec_ref, wicl_ref, w0_ref, a0_ref, kk_ref, ka_ref, rk_ref, s0_ref,
                y_ref, bonus_ref, sfin_ref, state_ref, *, reverse, n_chunks, n_pp):
    jt = pl.program_id(2)
    c = CHUNK
    order = list(range(n_chunks - 1, -1, -1) if reverse else range(n_chunks))
    per_wave = n_chunks // WKV_WAVES
    waves = [[(pi, ci) for ci in order[w * per_wave:(w + 1) * per_wave] for pi in range(n_pp)]
             for w in range(WKV_WAVES)]

    @pl.when(jt == 0)
    def _():
        state_ref[...] = jnp.zeros(state_ref.shape, F32)
        for hi in range(2 * n_pp):
            lo = (hi % 2) * HEAD_DIM
            state_ref[hi // 2, lo:lo + HEAD_DIM, lo:lo + HEAD_DIM] = s0_ref[0, hi]

    lane = lax.broadcasted_iota(jnp.int32, (PAIR, PAIR), 1)
    row = lax.broadcasted_iota(jnp.int32, (PAIR, PAIR), 0)
    own = (lane // HEAD_DIM) == (row // HEAD_DIM)
    head0_row = lax.broadcasted_iota(jnp.int32, (1, PAIR), 1) < HEAD_DIM
    if reverse:
        strict, incl = lane > row, lane >= row
    else:
        strict, incl = row > lane, row >= lane
    head0 = lax.broadcasted_iota(jnp.int32, (c, PAIR), 1) < HEAD_DIM

    def head_sum(x):
        s0 = jnp.sum(jnp.where(head0_row, x, 0.0), axis=-1, keepdims=True)
        s1 = jnp.sum(jnp.where(head0_row, 0.0, x), axis=-1, keepdims=True)
        return jnp.where(head0_row, s0, s1)

    def swap_heads(x):
        return pltpu.roll(x, HEAD_DIM, 1)

    def stack(x):
        return jnp.concatenate([jnp.where(head0, x, 0.0), jnp.where(head0, 0.0, x)], axis=0)

    def chunk_operands(units, src):
        r, k_d, v, a_vec, b_vec, log_decay = src
        tri = incl[:c, :c].astype(F32)
        cut = lambda x, u: x[u[1] * c:(u[1] + 1) * c, u[0] * PAIR:(u[0] + 1) * PAIR]
        lw = [cut(log_decay, u) for u in units]
        cum = [_dot_split_rhs(tri, x) for x in lw]
        tot = [x[0:1] if reverse else x[c - 1:c] for x in cum]
        g_in = [jnp.exp(x) for x in cum]
        g_ex = [jnp.exp(x - y) for x, y in zip(cum, lw)]
        g_inv = [jnp.exp(-x) for x in cum]
        g_tot = [jnp.exp(t) for t in tot]
        g_hat = [t * x for t, x in zip(g_tot, g_inv)]
        aa = [stack(cut(a_vec, u) * g) for u, g in zip(units, g_ex)]
        ra = [stack(cut(r, u) * g) for u, g in zip(units, g_in)]
        bb = [stack((cut(b_vec, u) * g).astype(BF16)) for u, g in zip(units, g_inv)]
        kt = [stack((cut(k_d, u) * g).astype(BF16)) for u, g in zip(units, g_inv)]
        bh = [stack((cut(b_vec, u) * g).astype(BF16)) for u, g in zip(units, g_hat)]
        kh = [stack((cut(k_d, u) * g).astype(BF16)) for u, g in zip(units, g_hat)]
        vm = [stack(cut(v, u)) for u in units]
        vsw = [swap_heads(w) for w in vm]
        vm = [w.astype(BF16) for w in vm]
        return g_tot, aa, ra, bb, kt, bh, kh, vm, vsw

    def chunk_local(ops):
        g_tot, aa, ra, bb, kt, bh, kh, vm, vsw = ops
        scores = [_dot_nt(jnp.concatenate([a, b], axis=0), jnp.concatenate([e, f], axis=0))
                  for a, b, e, f in zip(aa, ra, bb, kt)]
        n_pow = [jnp.where(strict, s[:PAIR, :PAIR], 0.0) for s in scores]
        n_ak = [jnp.where(strict, s[:PAIR, PAIR:], 0.0) for s in scores]
        m_rb = [jnp.where(incl, s[PAIR:, :PAIR], 0.0) for s in scores]
        m_rk = [jnp.where(incl, s[PAIR:, PAIR:], 0.0) for s in scores]
        xs = [a + swap_heads(_dot(n, w)) for a, n, w in zip(aa, n_ak, vm)]
        steps = CHUNK.bit_length() - 1
        for si in range(steps):
            if si + 1 < steps:
                n_pow = [n.astype(BF16) for n in n_pow]
                prod = [_dot(n, jnp.concatenate([x.astype(BF16), n], axis=1)) for n, x in zip(n_pow, xs)]
                xs = [x + p[:, :PAIR] for x, p in zip(xs, prod)]
                n_pow = [p[:, PAIR:] for p in prod]
            else:
                xs = [x + _dot(n, x) for n, x in zip(n_pow, xs)]
        xv = [jnp.concatenate([x, w], axis=0) for x, w in zip(xs, vsw)]
        oc = [_dot(jnp.concatenate([mb, mk], axis=1), w) for mb, mk, w in zip(m_rb, m_rk, xv)]
        rh = [x + jnp.where(own, o, 0.0) for x, o in zip(ra, oc)]
        y_loc = [swap_heads(jnp.where(own, 0.0, o)) for o in oc]
        rq = [_dot_tn(w, jnp.concatenate([b, k_], axis=0)) for w, b, k_ in zip(xv, bh, kh)]
        pp = [jnp.where(own, q, 0.0) for q in rq]
        qq = [jnp.where(own, jnp.concatenate([q[HEAD_DIM:], q[:HEAD_DIM]], axis=0), 0.0) for q in rq]
        return g_tot, pp, qq, rh, y_loc

    zs = zs_ref[...].astype(F32)
    w_lin = w0_ref[0] + _dot(jnp.tanh(zs), wdec_ref[0])
    log_decay = -jnp.exp(-0.5) * _sigmoid(w_lin)
    a_lr = _sigmoid(a0_ref[0] + _dot(zs, wicl_ref[0]))
    r = r_ref[...].astype(F32)
    k = k_ref[...].astype(F32)
    v = v_ref[...].astype(F32)
    kk = k * kk_ref[...]
    k_d = k * (1.0 + (a_lr - 1.0) * ka_ref[...])
    kk_sq = kk * kk
    rkr = r * k_d * rk_ref[...]
    lanes = lambda x, pi: x[:, pi * PAIR:(pi + 1) * PAIR]
    kk_norm = jnp.concatenate([jnp.sqrt(head_sum(lanes(kk_sq, pi))) for pi in range(n_pp)], axis=1)
    bonus = jnp.concatenate([head_sum(lanes(rkr, pi)) for pi in range(n_pp)], axis=1) * v
    bonus_ref[0] = bonus.astype(bonus_ref.dtype)
    kk = kk / jnp.maximum(kk_norm, L2_EPS)
    src = (r, k_d, v, -kk, kk * a_lr, log_decay)

    local = {}
    for units in waves:
        for u, vals in zip(units, zip(*chunk_local(chunk_operands(units, src)))):
            local[u] = vals

    state = [state_ref[pi] for pi in range(n_pp)]
    ys = {}
    for ci in order:
        for pi in range(n_pp):
            g_tot, pp, qq, rh, y_loc = local[(pi, ci)]
            ym = y_loc + _dot_nt(rh, state[pi])
            state[pi] = state[pi] * g_tot + _dot(state[pi], pp) + qq
            ys[(pi, ci)] = ym[:c] + ym[c:]
    y_ref[0] = jnp.concatenate(
        [jnp.concatenate([ys[(pi, ci)] for pi in range(n_pp)], axis=1) for ci in range(n_chunks)],
        axis=0).astype(y_ref.dtype)
    for pi in range(n_pp):
        state_ref[pi] = state[pi]

    @pl.when(jt == pl.num_programs(2) - 1)
    def _():
        for hi in range(2 * n_pp):
            lo = (hi % 2) * HEAD_DIM
            sfin_ref[0, hi] = state[hi // 2][lo:lo + HEAD_DIM, lo:lo + HEAD_DIM]


def _wkv(z, s0p, wdec, wicl, layer, w0, a0, k_k, k_a, r_k, *, n_seq, seq_len, tb, reverse, col_r, col_k, col_v,
         col_s):
    t = z.shape[0]
    n_pairs = k_k.shape[1] // PAIR
    nb = seq_len // tb
    d = 1 if reverse else 0
    n_pp = WKV_PAIRS_PER_STEP
    wl = n_pp * PAIR
    assert (tb // CHUNK) % 2 == 0 and n_pairs % n_pp == 0 and seq_len % tb == 0

    def tblk(b, j):
        return b * nb + ((nb - 1 - j) if reverse else j)

    kern = functools.partial(_wkv_kernel, reverse=reverse, n_chunks=tb // CHUNK, n_pp=n_pp)
    tok = lambda col: pl.BlockSpec((tb, wl), lambda b, p, j: (tblk(b, j), col // wl + p))
    vec = pl.BlockSpec((1, wl), lambda b, p, j: (0, p))
    dvec = pl.BlockSpec((1, 1, wl), lambda b, p, j: (d, 0, p))
    dmat = pl.BlockSpec((None, 1, R_SMALL, wl), lambda b, p, j: (layer, d, 0, p))
    st = pl.BlockSpec((1, 2 * n_pp, HEAD_DIM, HEAD_DIM), lambda b, p, j: (b, p, 0, 0))
    out_tok = pl.BlockSpec((1, tb, wl), lambda b, p, j: (0, tblk(b, j), p))
    y, bonus, sfin = pl.pallas_call(
        kern,
        grid=(n_seq, n_pairs // n_pp, nb),
        in_specs=[tok(col_r), tok(col_k), tok(col_v),
                  pl.BlockSpec((tb, R_SMALL), lambda b, p, j: (tblk(b, j), col_s // R_SMALL)),
                  dmat, dmat, dvec, dvec, vec, vec, vec, st],
        out_specs=[out_tok, out_tok, st],
        out_shape=[jax.ShapeDtypeStruct((1, t, n_pairs * PAIR), BF16),
                   jax.ShapeDtypeStruct((1, t, n_pairs * PAIR), BF16),
                   jax.ShapeDtypeStruct((n_seq, 2 * n_pairs, HEAD_DIM, HEAD_DIM), F32)],
        scratch_shapes=[pltpu.VMEM((n_pp, PAIR, PAIR), F32)],
        compiler_params=_cparams(),
        name="wkv_bwd" if reverse else "wkv_fwd",
    )(z, z, z, z, wdec, wicl, w0, a0, k_k, k_a, r_k, s0p)
    return y[0], bonus[0], sfin


def _shift_prev(x, period):
    t = lax.broadcasted_iota(jnp.int32, x.shape, 0)
    return jnp.where((t & (period - 1)) == 0, 0.0, pltpu.roll(x, 1, 0))


def _shift_next(x, period):
    t = lax.broadcasted_iota(jnp.int32, x.shape, 0)
    return jnp.where((t & (period - 1)) == period - 1, 0.0, pltpu.roll(x, x.shape[0] - 1, 0))


def _mix_kernel(x_ref, yf_ref, yb_ref, bf_ref, bb_ref, gd_ref, cb_ref, cc_ref, cx_ref, ga_ref, gb_ref, mod_ref,
                gnw_ref, gnb_ref, g2_ref, cw_ref, cbias_ref, wpa_ref, wpb_ref, wo_ref, o_ref,
                *, tm, seq_len, row0, period):
    i = pl.program_id(0)
    d = x_ref.shape[1]
    lane = lax.broadcasted_iota(jnp.int32, (PAIR, PAIR), 1)
    row = lax.broadcasted_iota(jnp.int32, (PAIR, PAIR), 0)
    head_mean = jnp.where((lane // HEAD_DIM) == (row // HEAD_DIM), 1.0 / HEAD_DIM, 0.0).astype(F32)

    y = yf_ref[...].astype(F32) + yb_ref[...].astype(F32)
    parts = []
    for s in range(d // PAIR):
        ys = y[:, s * PAIR:(s + 1) * PAIR]
        mu = _dot(ys, head_mean)
        yc = ys - mu
        var = _dot(yc * yc, head_mean)
        parts.append(yc * lax.rsqrt(var + GN_EPS))
    yn = jnp.concatenate(parts, axis=1)
    o = yn * gnw_ref[...] + gnb_ref[...] + bf_ref[...].astype(F32) + bb_ref[...].astype(F32)
    gate = _dot(_sigmoid(gd_ref[...].astype(F32)), g2_ref[...])
    o_a = o * gate

    p = cc_ref[...].astype(F32) * cx_ref[...].astype(F32)
    cw = cw_ref[...]
    conv = _shift_prev(p, period) * cw[0:1] + p * cw[1:2] + _shift_next(p, period) * cw[2:3] + cbias_ref[...]
    o_b = cb_ref[...].astype(F32) * conv

    merged = (_sigmoid(ga_ref[...].astype(F32)) * _dot(o_a, wpa_ref[...])
              + _sigmoid(gb_ref[...].astype(F32)) * _dot(o_b, wpb_ref[...]))
    mrow = _mod_row(i, tm, seq_len, row0)
    gt1 = mod_ref[pl.ds(mrow, 1), 2 * d:3 * d]
    o_ref[...] = x_ref[...] + gt1 * _dot(merged, wo_ref[...])


def _mix(x, yf, yb, bonf, bonb, z, mod_l, gn_w, gn_b, g2, conv_w, conv_b, w_pa, w_pb, w_o, layer,
         *, tm, seq_len, row0, period, cols):
    t, d = x.shape
    stacked = lambda a: pl.BlockSpec((None,) + a.shape[1:], lambda i: (layer,) + (0,) * (a.ndim - 1))
    kern = functools.partial(_mix_kernel, tm=tm, seq_len=seq_len, row0=row0, period=period)
    tok = pl.BlockSpec((tm, d), lambda i: (i, 0))
    zcol = lambda col: pl.BlockSpec((tm, d), lambda i: (i, col // d))
    full = lambda a: pl.BlockSpec(a.shape, lambda i: (0,) * a.ndim)
    vec = pl.BlockSpec((1, d), lambda i: (0, 0))
    return pl.pallas_call(
        kern,
        grid=(t // tm,),
        in_specs=[tok, tok, tok, tok, tok,
                  pl.BlockSpec((tm, R_G), lambda i: (i, cols["gd"] // R_G)),
                  zcol(cols["cb"]), zcol(cols["cc"]), zcol(cols["cx"]), zcol(cols["ga"]), zcol(cols["gb"]),
                  full(mod_l), vec, vec, stacked(g2), full(conv_w), vec, stacked(w_pa), stacked(w_pb),
                  stacked(w_o)],
        out_specs=tok,
        out_shape=jax.ShapeDtypeStruct((t, d), F32),
        compiler_params=_cparams(),
        name="mix",
    )(x, yf, yb, bonf, bonb, z, z, z, z, z, z, mod_l, gn_w.reshape(1, d), gn_b.reshape(1, d), g2, conv_w,
      conv_b.reshape(1, d), w_pa, w_pb, w_o)


def _ffn_kernel(x_ref, u_ref, up_ref, un_ref, mod_ref, cw_ref, cb_ref, wd_ref, gf_ref, o_ref,
                *, tm, seq_len, row0, grid, final, kc):
    i = pl.program_id(0)
    d = x_ref.shape[1]
    d_ff = wd_ref.shape[0]
    tiles_per_seq = seq_len // tm
    first = (i % tiles_per_seq) == 0
    last = (i % tiles_per_seq) == tiles_per_seq - 1

    def conv(col):
        u = u_ref[:, col:col + kc].astype(F32)
        if grid:
            halo_p = jnp.where(first, 0.0, up_ref[:, col:col + kc].astype(F32))
            halo_n = jnp.where(last, 0.0, un_ref[:, col:col + kc].astype(F32))
            prev = jnp.concatenate([halo_p, u[:tm - GRID_W]], axis=0)
            nxt = jnp.concatenate([u[GRID_W:], halo_n], axis=0)
        else:
            prev = _shift_prev(u, seq_len)
            nxt = _shift_next(u, seq_len)
        cw = cw_ref[:, col:col + kc]
        return prev * cw[0:1] + u * cw[1:2] + nxt * cw[2:3] + cb_ref[:, col:col + kc]

    acc = jnp.zeros((tm, d), F32)
    for ci in range(d_ff // kc):
        act = conv(ci * kc)
        lin = conv(d_ff + ci * kc)
        acc = acc + _dot(_silu(act.astype(BF16)) * lin.astype(BF16), wd_ref[ci * kc:(ci + 1) * kc, :])
    mrow = _mod_row(i, tm, seq_len, row0)
    gt2 = mod_ref[pl.ds(mrow, 1), 5 * d:6 * d]
    xn = x_ref[...] + gt2 * acc
    if final:
        xn = xn * lax.rsqrt(jnp.mean(xn * xn, axis=-1, keepdims=True) + NORM_EPS) * gf_ref[...]
    o_ref[...] = xn


def _ffn(x, u, mod_l, conv_w, conv_b, w_down, layer, norm_f_g, *, tm, seq_len, row0, grid, final):
    t, d = x.shape
    n_up = u.shape[1]
    hb = tm // GRID_W
    n_halo = t // GRID_W
    kern = functools.partial(_ffn_kernel, tm=tm, seq_len=seq_len, row0=row0, grid=grid, final=final,
                             kc=FFN_COL_CHUNK)
    full = lambda a: pl.BlockSpec(a.shape, lambda i: (0,) * a.ndim)
    return pl.pallas_call(
        kern,
        grid=(t // tm,),
        in_specs=[pl.BlockSpec((tm, d), lambda i: (i, 0)),
                  pl.BlockSpec((tm, n_up), lambda i: (i, 0)),
                  pl.BlockSpec((GRID_W, n_up), lambda i: (jnp.maximum(i * hb - 1, 0), 0)),
                  pl.BlockSpec((GRID_W, n_up), lambda i: (jnp.minimum(i * hb + hb, n_halo - 1), 0)),
                  full(mod_l), full(conv_w), pl.BlockSpec((1, n_up), lambda i: (0, 0)),
                  pl.BlockSpec((None,) + w_down.shape[1:], lambda i: (layer, 0, 0)),
                  pl.BlockSpec((1, d), lambda i: (0, 0))],
        out_specs=pl.BlockSpec((tm, d), lambda i: (i, 0)),
        out_shape=jax.ShapeDtypeStruct((t, d), F32),
        compiler_params=_cparams(),
        name="ffn",
    )(x, u, u, u, mod_l, conv_w, conv_b.reshape(1, n_up), w_down, norm_f_g.reshape(1, d))


def kernel(x_prompt, x_sample, state_wkv, c, c_ctx, w_mod, b_mod, norm1_g, w_in, decay_w0, decay_w2, iclr_a0,
           iclr_a2, gate_g2, k_k, k_a, r_k, gn_w, gn_b, conv_mix_w, conv_mix_b, w_pa, w_pb, w_o, norm2_g, w_up,
           conv_ffn_w, conv_ffn_b, w_down, norm_f_g):
    n_ctx, ctx_len, d = x_prompt.shape
    n_lat, lat_len, _ = x_sample.shape
    n_layers = w_mod.shape[0]
    d_a = k_k.shape[1]
    n_heads = d_a // HEAD_DIM
    r_w = decay_w2.shape[2]
    assert d_a == d and 4 * r_w == R_SMALL and gate_g2.shape[1] == R_G

    o_small = 3 * d_a
    o_gd = o_small + R_SMALL
    o_cb = o_gd + R_G
    o_gates = o_cb + 3 * d
    cols = dict(r=0, k=d, v=2 * d, cb=3 * d, cc=4 * d, cx=5 * d, ga=6 * d, gb=7 * d, small=8 * d,
                gd=8 * d + R_SMALL)
    n_in = 8 * d + R_SMALL + R_G
    n_in_pad = 8 * d + 512
    w_in_p = jnp.concatenate(
        [w_in[:, :, :o_small], w_in[:, :, o_cb:o_gates], w_in[:, :, o_gates:], w_in[:, :, o_small:o_cb],
         jnp.zeros((n_layers, d, n_in_pad - n_in), w_in.dtype)], axis=2).astype(BF16)
    w_up_b = w_up.astype(BF16)
    w_pa_b, w_pb_b, w_o_b = w_pa.astype(BF16), w_pb.astype(BF16), w_o.astype(BF16)
    w_down_b = w_down.astype(BF16)
    g2_b = gate_g2.astype(BF16)

    zrow = jnp.zeros((n_layers, r_w, d_a), F32)
    wdec = jnp.stack([jnp.concatenate([decay_w2[:, 0], zrow, zrow, zrow], axis=1),
                      jnp.concatenate([zrow, decay_w2[:, 1], zrow, zrow], axis=1)], axis=1)
    wicl = jnp.stack([jnp.concatenate([zrow, zrow, iclr_a2[:, 0], zrow], axis=1),
                      jnp.concatenate([zrow, zrow, zrow, iclr_a2[:, 1]], axis=1)], axis=1)

    cond8 = jnp.concatenate([c_ctx[None, :], c, jnp.zeros((8 - 1 - n_lat, d), F32)], axis=0)
    mod = _modulation(cond8, w_mod, b_mod)

    def layer(x, l, *, n_seq, seq_len, row0, grid, s0, final):
        tm = TOKEN_TILE
        z = _norm_mm(x, norm1_g[l], mod[l], w_in_p, l, tm=MM_TOKEN_TILE, tn=n_in_pad // MM_COL_TILES, seq_len=seq_len,
                     row0=row0, sc_col=1, sh_col=0, name="in_proj")
        outs = []
        for rev in (False, True):
            dd = 1 if rev else 0
            outs.append(_wkv(z, s0[dd], wdec, wicl, l, decay_w0[l].reshape(2, 1, d_a),
                             iclr_a0[l].reshape(2, 1, d_a), k_k[l].reshape(1, d_a), k_a[l].reshape(1, d_a),
                             r_k[l].reshape(1, d_a), n_seq=n_seq, seq_len=seq_len, tb=WKV_BLOCK, reverse=rev,
                             col_r=cols["r"], col_k=cols["k"], col_v=cols["v"], col_s=cols["small"]))
        (yf, bonf, sf), (yb, bonb, sb) = outs
        x = _mix(x, yf, yb, bonf, bonb, z, mod[l], gn_w[l], gn_b[l], g2_b, conv_mix_w[l], conv_mix_b[l],
                 w_pa_b, w_pb_b, w_o_b, l, tm=tm, seq_len=seq_len, row0=row0,
                 period=GRID_W if grid else seq_len, cols=cols)
        u = _norm_mm(x, norm2_g[l], mod[l], w_up_b, l, tm=MM_TOKEN_TILE, tn=w_up.shape[2] // MM_COL_TILES, seq_len=seq_len,
                     row0=row0, sc_col=4, sh_col=3, name="ffn_up")
        x = _ffn(x, u, mod[l], conv_ffn_w[l], conv_ffn_b[l], w_down_b, l, norm_f_g, tm=tm, seq_len=seq_len,
                 row0=row0, grid=grid, final=final)
        return x, sf, sb

    xp = x_prompt.reshape(n_ctx * ctx_len, d)
    zero_state = jnp.zeros((n_ctx, n_heads, HEAD_DIM, HEAD_DIM), F32)
    ctx_states = []
    for l in range(n_layers):
        xp, sf, sb = layer(xp, l, n_seq=n_ctx, seq_len=ctx_len, row0=0, grid=False,
                           s0=(zero_state, zero_state), final=(l == n_layers - 1))
        ctx_states.append(jnp.stack([sf, sb], axis=1))
    new_state = jnp.stack(ctx_states, axis=1).astype(x_prompt.dtype)
    y_prompt = xp.reshape(n_ctx, ctx_len, d)

    xs = x_sample.reshape(n_lat * lat_len, d)
    for l in range(n_layers):
        s0 = (state_wkv[:, l, 0].astype(F32), state_wkv[:, l, 1].astype(F32))
        xs, _, _ = layer(xs, l, n_seq=n_lat, seq_len=lat_len, row0=1, grid=True, s0=s0,
                         final=(l == n_layers - 1))
    y_sample = xs.reshape(n_lat, lat_len, d)
    return (y_prompt, y_sample, new_state)
```
